```python
import math
import jax, jax.numpy as jnp
from jax import lax
import numpy as np


D_MODEL = 1024
BATCH = 8
SEQ = 4096
DEPTH = 4

CHUNK = 64
Q_BLOCK = 128
MEM_LEN = 256

MLA_HEADS = 8
MLA_NOPE = 64
MLA_ROPE = 32
MLA_V = 64
MLA_QK = MLA_NOPE + MLA_ROPE
Q_LORA = 384
KV_LORA = 256
ROPE_BASE = 10000.0

DIFF_HEADS = 4
DIFF_QK = 64
DIFF_V = 2 * DIFF_QK

MEM_HEADS = 4
MEM_HEAD_DIM = 128

BRANCH_W = 512
N_BRANCH = 3
D_FF = 4 * D_MODEL

T5_BUCKETS = 32
T5_MAX_DIST = 128

EPS = 1e-6
NEG = -1e30

SPLITS = [Q_LORA, KV_LORA, MLA_ROPE,
          DIFF_HEADS * 2 * DIFF_QK, DIFF_HEADS * 2 * DIFF_QK, DIFF_HEADS * DIFF_V,
          MEM_HEADS * MEM_HEAD_DIM, N_BRANCH * D_MODEL]
D_IN = sum(SPLITS)

kernel_name = 'hybrid_mla_diffattn_memxattn_gated_block'


def rmsnorm(x, g):
    x32 = x.astype(jnp.float32)
    y = x32 * lax.rsqrt(jnp.mean(x32 * x32, axis=-1, keepdims=True) + EPS)
    return (y * g.astype(jnp.float32)).astype(x.dtype)


def rope(x, positions):
    half = MLA_ROPE // 2
    inv = jnp.power(jnp.float32(ROPE_BASE), -jnp.arange(half, dtype=jnp.float32) / half)
    ang = positions.astype(jnp.float32)[..., None] * inv
    ang = ang.reshape(ang.shape[:2] + (1,) * (x.ndim - 3) + (half,))
    cos, sin = jnp.cos(ang), jnp.sin(ang)
    x32 = x.astype(jnp.float32)
    x1, x2 = x32[..., :half], x32[..., half:]
    out = jnp.concatenate([x1 * cos - x2 * sin, x2 * cos + x1 * sin], axis=-1)
    return out.astype(x.dtype)


def t5_bucket(rel):
    n = T5_BUCKETS // 2
    ret = jnp.where(rel > 0, n, 0)
    a = jnp.abs(rel)
    max_exact = n // 2
    af = jnp.maximum(a, 1).astype(jnp.float32)
    large = max_exact + (jnp.log(af / max_exact) / math.log(T5_MAX_DIST / max_exact)
                         * (n - max_exact)).astype(jnp.int32)
    large = jnp.minimum(large, n - 1)
    return ret + jnp.where(a < max_exact, a, large)


def to_blocks(t):
    b, s = t.shape[0], t.shape[1]
    t = t.reshape((b, s // Q_BLOCK, Q_BLOCK) + t.shape[2:])
    return jnp.moveaxis(t, 1, 0)


def from_blocks(t):
    t = jnp.moveaxis(t, 0, 1)
    return t.reshape((t.shape[0], t.shape[1] * t.shape[2]) + t.shape[3:])


def chunk_mask(blk, seq):
    q_chunk = (blk * Q_BLOCK + jnp.arange(Q_BLOCK)) // CHUNK
    k_chunk = jnp.arange(seq) // CHUNK
    return k_chunk[None, :] <= q_chunk[:, None]


def mla_attention(q, k, v):
    seq = q.shape[1]
    scale = MLA_QK ** -0.5

    def one(args):
        qb, blk = args
        s = jnp.einsum('bqhd,bkhd->bhqk', qb, k, preferred_element_type=jnp.float32) * scale
        s = jnp.where(chunk_mask(blk, seq)[None, None], s, NEG)
        p = jax.nn.softmax(s, axis=-1)
        return jnp.einsum('bhqk,bkhd->bqhd', p.astype(v.dtype), v)

    o = lax.map(one, (to_blocks(q), jnp.arange(seq // Q_BLOCK)))
    return from_blocks(o)


def diff_attention(q, k, v, positions, t5_table, lam):
    seq = q.shape[1]
    scale = DIFF_QK ** -0.5
    table = t5_table.astype(jnp.float32)

    def one(args):
        qb, pq, blk = args
        s = jnp.einsum('bqhcd,bkhcd->bhcqk', qb, k, preferred_element_type=jnp.float32) * scale
        rel = positions[:, None, :] - pq[:, :, None]
        bias = jnp.moveaxis(table[t5_bucket(rel)], -1, 1)
        s = s + bias[:, :, None]
        s = jnp.where(chunk_mask(blk, seq)[None, None, None], s, NEG)
        p = jax.nn.softmax(s, axis=-1)
        pd = p[:, :, 0] - lam * p[:, :, 1]
        return jnp.einsum('bhqk,bkhd->bqhd', pd.astype(v.dtype), v)

    o = lax.map(one, (to_blocks(q), to_blocks(positions), jnp.arange(seq // Q_BLOCK)))
    return from_blocks(o)


def cross_attention(q, km, vm):
    scale = MEM_HEAD_DIM ** -0.5
    s = jnp.einsum('bqhd,bmhd->bhqm', q, km, preferred_element_type=jnp.float32) * scale
    p = jax.nn.softmax(s, axis=-1)
    return jnp.einsum('bhqm,bmhd->bqhd', p.astype(vm.dtype), vm)


def setup_inputs(seed: int = 0) -> dict:
    key = jax.random.key(seed)
    ks = jax.random.split(key, 32)
    f32 = jnp.float32

    def dense(k, shape, fan_in, scale=1.0):
        return jax.random.normal(k, shape, f32) * (scale * fan_in ** -0.5)

    def gain(k, shape):
        return 1.0 + 0.02 * jax.random.normal(k, shape, f32)

    start = jax.random.randint(ks[2], (BATCH, 1), 0, 4096, dtype=jnp.int32)
    positions = start + jnp.arange(SEQ, dtype=jnp.int32)[None, :]
    return {
        'x': jax.random.normal(ks[0], (BATCH, SEQ, D_MODEL), f32),
        'mem': jax.random.normal(ks[1], (BATCH, MEM_LEN, D_MODEL), f32),
        'positions': positions,
        't5_table': 0.5 * jax.random.normal(ks[3], (T5_BUCKETS, DIFF_HEADS), f32),
        'g_mix': gain(ks[4], (DEPTH, D_MODEL)),
        'g_mem': gain(ks[5], (DEPTH, D_MODEL)),
        'w_in': dense(ks[6], (DEPTH, D_MODEL, D_IN), D_MODEL),
        'g_cq': gain(ks[7], (DEPTH, Q_LORA)),
        'w_uq': dense(ks[8], (DEPTH, Q_LORA, MLA_HEADS * MLA_QK), Q_LORA),
        'g_ckv': gain(ks[9], (DEPTH, KV_LORA)),
        'w_ukv': dense(ks[10], (DEPTH, KV_LORA, MLA_HEADS * (MLA_NOPE + MLA_V)), KV_LORA),
        'g_mla_q': gain(ks[11], (DEPTH, MLA_QK)),
        'g_mla_k': gain(ks[12], (DEPTH, MLA_QK)),
        'g_diff_q': gain(ks[13], (DEPTH, DIFF_QK)),
        'g_diff_k': gain(ks[14], (DEPTH, DIFF_QK)),
        'lam_q1': 0.1 * jax.random.normal(ks[15], (DEPTH, DIFF_QK), f32),
        'lam_k1': 0.1 * jax.random.normal(ks[16], (DEPTH, DIFF_QK), f32),
        'lam_q2': 0.1 * jax.random.normal(ks[17], (DEPTH, DIFF_QK), f32),
        'lam_k2': 0.1 * jax.random.normal(ks[18], (DEPTH, DIFF_QK), f32),
        'g_diff_out': gain(ks[19], (DEPTH, DIFF_V)),
        'w_mem_kv': dense(ks[20], (DEPTH, D_MODEL, 2 * MEM_HEADS * MEM_HEAD_DIM), D_MODEL),
        'g_mem_q': gain(ks[21], (DEPTH, MEM_HEAD_DIM)),
        'g_mem_k': gain(ks[22], (DEPTH, MEM_HEAD_DIM)),
        'w_branch': dense(ks[23], (DEPTH, N_BRANCH, BRANCH_W, D_MODEL), BRANCH_W),
        'w_out': dense(ks[24], (DEPTH, D_MODEL, D_MODEL), D_MODEL, 0.5),
        'g_mlp': gain(ks[25], (DEPTH, D_MODEL)),
        'w_ff1': dense(ks[26], (DEPTH, D_MODEL, D_FF), D_MODEL),
        'w_ff2': dense(ks[27], (DEPTH, D_FF, D_MODEL), D_FF, 0.5),
    }


def reference(x, mem, positions, t5_table, g_mix, g_mem, w_in, g_cq, w_uq, g_ckv, w_ukv,
              g_mla_q, g_mla_k, g_diff_q, g_diff_k, lam_q1, lam_k1, lam_q2, lam_k2,
              g_diff_out, w_mem_kv, g_mem_q, g_mem_k, w_branch, w_out, g_mlp, w_ff1, w_ff2):
    b, s, _ = x.shape
    m = mem.shape[1]
    split_points = np.cumsum(SPLITS)[:-1].tolist()
    for l in range(DEPTH):
        h = rmsnorm(x, g_mix[l])
        z = h @ w_in[l]
        c_q, c_kv, k_r, dq, dk, dv, mq, gl = jnp.split(z, split_points, axis=-1)

        q = (rmsnorm(c_q, g_cq[l]) @ w_uq[l]).reshape(b, s, MLA_HEADS, MLA_QK)
        kv = (rmsnorm(c_kv, g_ckv[l]) @ w_ukv[l]).reshape(b, s, MLA_HEADS, MLA_NOPE + MLA_V)
        k_nope, v_a = kv[..., :MLA_NOPE], kv[..., MLA_NOPE:]
        k = jnp.concatenate(
            [k_nope, jnp.broadcast_to(k_r[:, :, None, :], (b, s, MLA_HEADS, MLA_ROPE))], axis=-1)
        q = rmsnorm(q, g_mla_q[l])
        k = rmsnorm(k, g_mla_k[l])
        q = jnp.concatenate([q[..., :MLA_NOPE], rope(q[..., MLA_NOPE:], positions)], axis=-1)
        k = jnp.concatenate([k[..., :MLA_NOPE], rope(k[..., MLA_NOPE:], positions)], axis=-1)
        o_a = mla_attention(q, k, v_a).reshape(b, s, BRANCH_W)

        lam_init = 0.8 - 0.6 * math.exp(-0.3 * l)
        lam = (jnp.exp(jnp.sum(lam_q1[l].astype(jnp.float32) * lam_k1[l].astype(jnp.float32)))
               - jnp.exp(jnp.sum(lam_q2[l].astype(jnp.float32) * lam_k2[l].astype(jnp.float32)))
               + lam_init)
        dq = rmsnorm(dq.reshape(b, s, DIFF_HEADS, 2, DIFF_QK), g_diff_q[l])
        dk = rmsnorm(dk.reshape(b, s, DIFF_HEADS, 2, DIFF_QK), g_diff_k[l])
        dv = dv.reshape(b, s, DIFF_HEADS, DIFF_V)
        o_b = diff_attention(dq, dk, dv, positions, t5_table, lam)
        o_b = (rmsnorm(o_b, g_diff_out[l]) * (1.0 - lam_init)).reshape(b, s, BRANCH_W)

        hm = rmsnorm(mem, g_mem[l])
        mkv = (hm @ w_mem_kv[l]).reshape(b, m, 2, MEM_HEADS, MEM_HEAD_DIM)
        km = rmsnorm(mkv[:, :, 0], g_mem_k[l])
        vm = mkv[:, :, 1]
        mq = rmsnorm(mq.reshape(b, s, MEM_HEADS, MEM_HEAD_DIM), g_mem_q[l])
        o_c = cross_attention(mq, km, vm).reshape(b, s, BRANCH_W)

        gates = jax.nn.sigmoid(gl.reshape(b, s, N_BRANCH, D_MODEL))
        y = (gates[:, :, 0] * (o_a @ w_branch[l, 0])
             + gates[:, :, 1] * (o_b @ w_branch[l, 1])
             + gates[:, :, 2] * (o_c @ w_branch[l, 2]))
        x = x + y @ w_out[l]

        h2 = rmsnorm(x, g_mlp[l])
        x = x + jnp.square(jax.nn.relu(h2 @ w_ff1[l])) @ w_ff2[l]
    return x
```

```python
import functools
import math

import jax
import jax.numpy as jnp
import numpy as np
from jax import lax
from jax.experimental import pallas as pl
from jax.experimental.pallas import tpu as pltpu

D_MODEL = 1024
DEPTH = 4
CHUNK = 64
MLA_HEADS = 8
MLA_NOPE = 64
MLA_ROPE = 32
MLA_V = 64
MLA_QK = MLA_NOPE + MLA_ROPE
Q_LORA = 384
KV_LORA = 256
ROPE_BASE = 10000.0
DIFF_HEADS = 4
DIFF_QK = 64
DIFF_V = 2 * DIFF_QK
MEM_HEADS = 4
MEM_HEAD_DIM = 128
BRANCH_W = 512
N_BRANCH = 3
D_FF = 4 * D_MODEL
T5_BUCKETS = 32
T5_MAX_DIST = 128
EPS = 1e-6
NEG = -1e30

LANES = 128
V7X_VMEM_BYTES = 64 * 1024 * 1024
VMEM_LIMIT = V7X_VMEM_BYTES * 7 // 8

TOKEN_TILE = 512
ATTN_TILE = 512
TABLE_TILE = 2048
FF_CHUNK = 1024

ROPE_LO = MLA_NOPE
ROPE_MID = MLA_NOPE + MLA_ROPE // 2
ROPE_HI = MLA_NOPE + MLA_ROPE

_C_CQ = 0
_C_CKV = _C_CQ + Q_LORA
_C_KR = _C_CKV + KV_LORA
_C_DQ = _C_KR + LANES
_C_DK = _C_DQ + DIFF_HEADS * LANES
_C_DV = _C_DK + DIFF_HEADS * LANES
_C_MQ = _C_DV + DIFF_HEADS * DIFF_V
_C_END = _C_MQ + MEM_HEADS * MEM_HEAD_DIM


def _bf16(x):
    return x.astype(jnp.bfloat16)


def _dot(a, b):
    return jnp.dot(a, b, preferred_element_type=jnp.float32)


def _dot_nt(a, b):
    return lax.dot_general(a, b, (((1,), (1,)), ((), ())), preferred_element_type=jnp.float32)


def _rms_scale(x, n):
    return lax.rsqrt(jnp.sum(x * x, axis=-1, keepdims=True) * (1.0 / n) + EPS)


def _const_spec(shape):
    return pl.BlockSpec(shape, lambda *_: (0,) * len(shape))


def _params(*sem):
    return pltpu.CompilerParams(dimension_semantics=sem, vmem_limit_bytes=VMEM_LIMIT)


def _rope_tables_kernel(pos_ref, inv_ref, cos_ref, sina_ref, sinb_ref):
    ang = pos_ref[...].astype(jnp.float32) * inv_ref[...]
    lane = lax.broadcasted_iota(jnp.int32, ang.shape, 1)
    first = (lane >= ROPE_LO) & (lane < ROPE_MID)
    second = (lane >= ROPE_MID) & (lane < ROPE_HI)
    c = jnp.cos(ang)
    s = jnp.sin(ang)
    cos_ref[...] = jnp.where(first | second, c, 1.0)
    sina_ref[...] = jnp.where(second, s, 0.0)
    sinb_ref[...] = jnp.where(first, -s, 0.0)


def _rope_tables(positions):
    t = positions.size
    half = MLA_ROPE // 2
    inv = jnp.power(jnp.float32(ROPE_BASE), -jnp.arange(half, dtype=jnp.float32) / half)
    inv_full = jnp.zeros((1, LANES), jnp.float32)
    inv_full = inv_full.at[0, ROPE_LO:ROPE_MID].set(inv).at[0, ROPE_MID:ROPE_HI].set(inv)
    out = jax.ShapeDtypeStruct((t, LANES), jnp.float32)
    return pl.pallas_call(
        _rope_tables_kernel,
        grid=(t // TABLE_TILE,),
        in_specs=[pl.BlockSpec((TABLE_TILE, 1), lambda i: (i, 0)), _const_spec((1, LANES))],
        out_specs=[pl.BlockSpec((TABLE_TILE, LANES), lambda i: (i, 0))] * 3,
        out_shape=[out] * 3,
        compiler_params=_params("arbitrary"),
        name="rope_tables",
    )(positions.reshape(t, 1), inv_full)


def _t5_large_thresholds():
    n = T5_BUCKETS // 2
    max_exact = n // 2
    assert T5_MAX_DIST == 16 * max_exact and n - max_exact == 8
    out = []
    for j in range(1, n - max_exact):
        a = max_exact
        while a * a < max_exact * max_exact * 2 ** j:
            a += 1
        out.append(a)
    return out


def _t5_bias_kernel(table_ref, bias_ref):
    t = bias_ref.shape[-1]
    off = pl.program_id(0) * t
    row = lax.broadcasted_iota(jnp.int32, (t, t), 0)
    col = lax.broadcasted_iota(jnp.int32, (t, t), 1)
    rel = col - row - off
    n = T5_BUCKETS // 2
    max_exact = n // 2
    a = jnp.abs(rel)
    large = jnp.full((t, t), max_exact, jnp.int32)
    for thr in _t5_large_thresholds():
        large = large + (a >= thr).astype(jnp.int32)
    bucket = jnp.where(rel > 0, n, 0) + jnp.where(a < max_exact, a, large)
    far_bucket = n - 1
    for h in range(DIFF_HEADS):
        val = jnp.full((t, t), table_ref[0, h], jnp.float32)
        for b in range(1, T5_BUCKETS):
            val = jnp.where(bucket == b, table_ref[b, h], val)
        bias_ref[0, h] = val - table_ref[far_bucket, h]


def _t5_bias_tiles(t5_table):
    t = ATTN_TILE
    assert t >= _t5_large_thresholds()[-1]
    return pl.pallas_call(
        _t5_bias_kernel,
        grid=(2,),
        in_specs=[pl.BlockSpec(memory_space=pltpu.SMEM)],
        out_specs=pl.BlockSpec((1, DIFF_HEADS, t, t), lambda i: (i, 0, 0, 0)),
        out_shape=jax.ShapeDtypeStruct((2, DIFF_HEADS, t, t), jnp.float32),
        compiler_params=_params("arbitrary"),
        name="t5_bias_tiles",
    )(t5_table.astype(jnp.float32))


def _mem_kv_kernel(mem_ref, g_ref, w_ref, gk_ref, km_ref, vm_ref):
    x = mem_ref[0]
    h = _bf16(x * _rms_scale(x, D_MODEL) * g_ref[0])
    kv = _dot(h, w_ref[0])
    width = MEM_HEADS * MEM_HEAD_DIM
    for hd in range(MEM_HEADS):
        sl = slice(hd * MEM_HEAD_DIM, (hd + 1) * MEM_HEAD_DIM)
        k = kv[:, sl]
        km_ref[0, 0, :, sl] = _bf16(k * _rms_scale(k, MEM_HEAD_DIM) * gk_ref[0])
    vm_ref[0, 0] = _bf16(kv[:, width:])


def _mem_kv(mem, g_mem, w_mem_kv, g_mem_k):
    b, m, _ = mem.shape
    width = MEM_HEADS * MEM_HEAD_DIM
    out = jax.ShapeDtypeStruct((DEPTH, b, m, width), jnp.bfloat16)
    return pl.pallas_call(
        _mem_kv_kernel,
        grid=(DEPTH, b),
        in_specs=[
            pl.BlockSpec((1, m, D_MODEL), lambda l, i: (i, 0, 0)),
            pl.BlockSpec((1, 1, D_MODEL), lambda l, i: (l, 0, 0)),
            pl.BlockSpec((1, D_MODEL, 2 * width), lambda l, i: (l, 0, 0)),
            pl.BlockSpec((1, 1, MEM_HEAD_DIM), lambda l, i: (l, 0, 0)),
        ],
        out_specs=[pl.BlockSpec((1, 1, m, width), lambda l, i: (l, i, 0, 0))] * 2,
        out_shape=[out] * 2,
        compiler_params=_params("arbitrary", "arbitrary"),
        name="mem_kv",
    )(mem, g_mem.reshape(DEPTH, 1, D_MODEL), _bf16(w_mem_kv), g_mem_k.reshape(DEPTH, 1, MEM_HEAD_DIM))


def _rope(x, cos_t, sina_t, sinb_t):
    half = MLA_ROPE // 2
    return x * cos_t + pltpu.roll(x, half, 1) * sina_t + pltpu.roll(x, LANES - half, 1) * sinb_t


def _proj_kernel(x_ref, gmix_ref, win_ref, gcq_ref, wuq_ref, gckv_ref, wuk_ref, wuv_ref,
                 gq_ref, gk_ref, cos_ref, sina_ref, sinb_ref, gdq_ref, gdk_ref, gmq_ref,
                 km_ref, vm_ref,
                 q_ref, k_ref, v_ref, dq0_ref, dq1_ref, dk_ref, dv_ref, oc_ref):
    x = x_ref[...]
    h = _bf16(x * _rms_scale(x, D_MODEL) * gmix_ref[...])
    z = _dot(h, win_ref[...])
    cos_t, sina_t, sinb_t = cos_ref[...], sina_ref[...], sinb_ref[...]

    c_q = z[:, _C_CQ:_C_CKV]
    q_raw = _dot(_bf16(c_q * _rms_scale(c_q, Q_LORA) * gcq_ref[...]), wuq_ref[...])
    c_kv = z[:, _C_CKV:_C_KR]
    ckv_n = _bf16(c_kv * _rms_scale(c_kv, KV_LORA) * gckv_ref[...])
    k_nope = _dot(ckv_n, wuk_ref[...])
    v_ref[...] = _bf16(_dot(ckv_n, wuv_ref[...]))
    k_rope = z[:, _C_KR:_C_DQ]
    gq, gk = gq_ref[...], gk_ref[...]
    for hd in range(MLA_HEADS):
        sl = slice(hd * LANES, (hd + 1) * LANES)
        q = q_raw[:, sl]
        q_ref[:, sl] = _bf16(_rope(q * _rms_scale(q, MLA_QK) * gq, cos_t, sina_t, sinb_t))
        k = k_nope[:, sl] + k_rope
        k_ref[:, sl] = _bf16(_rope(k * _rms_scale(k, MLA_QK) * gk, cos_t, sina_t, sinb_t))

    lane = lax.broadcasted_iota(jnp.int32, (x.shape[0], LANES), 1)
    lo = lane < DIFF_QK
    gdq, gdk = gdq_ref[...], gdk_ref[...]

    def half_norm(t, g):
        t2 = t * t
        s_lo = jnp.sum(jnp.where(lo, t2, 0.0), axis=-1, keepdims=True)
        s_hi = jnp.sum(jnp.where(lo, 0.0, t2), axis=-1, keepdims=True)
        ms = jnp.where(lo, s_lo, s_hi) * (1.0 / DIFF_QK)
        return t * lax.rsqrt(ms + EPS) * g

    for hd in range(DIFF_HEADS):
        sl = slice(hd * LANES, (hd + 1) * LANES)
        dq = half_norm(z[:, _C_DQ + hd * LANES:_C_DQ + (hd + 1) * LANES], gdq)
        dq0_ref[:, sl] = _bf16(jnp.where(lo, dq, 0.0))
        dq1_ref[:, sl] = _bf16(jnp.where(lo, 0.0, dq))
        dk_ref[:, sl] = _bf16(half_norm(z[:, _C_DK + hd * LANES:_C_DK + (hd + 1) * LANES], gdk))
    dv_ref[...] = _bf16(z[:, _C_DV:_C_MQ])

    gmq = gmq_ref[...]
    for hd in range(MEM_HEADS):
        sl = slice(hd * MEM_HEAD_DIM, (hd + 1) * MEM_HEAD_DIM)
        mq = z[:, _C_MQ + hd * MEM_HEAD_DIM:_C_MQ + (hd + 1) * MEM_HEAD_DIM]
        mq = _bf16(mq * _rms_scale(mq, MEM_HEAD_DIM) * gmq)
        s = _dot_nt(mq, km_ref[0, 0, :, sl])
        p = jnp.exp(s - jnp.max(s, axis=-1, keepdims=True))
        o = _dot(_bf16(p), vm_ref[0, 0, :, sl])
        oc_ref[:, sl] = _bf16(o / jnp.sum(p, axis=-1, keepdims=True))


def _proj(layer, x2, w, tabs, km, vm, seq):
    t = x2.shape[0]
    tm = TOKEN_TILE
    steps_per_batch = seq // tm
    m = km.shape[2]
    width = MEM_HEADS * MEM_HEAD_DIM
    row = lambda i: (i, 0)
    tok = lambda c: pl.BlockSpec((tm, c), row)
    mem_spec = pl.BlockSpec((1, 1, m, width), lambda i: (layer, i // steps_per_batch, 0, 0))
    wide = jax.ShapeDtypeStruct((t, MLA_HEADS * LANES), jnp.bfloat16)
    narrow = jax.ShapeDtypeStruct((t, BRANCH_W), jnp.bfloat16)
    consts = [w["g_mix"], w["w_in"], w["g_cq"], w["w_uq"], w["g_ckv"], w["w_uk"], w["w_uv"],
              w["g_q"], w["g_k"]]
    gains = [w["g_dq"], w["g_dk"], w["g_mq"]]
    return pl.pallas_call(
        _proj_kernel,
        grid=(t // tm,),
        in_specs=([tok(D_MODEL)] + [_const_spec(c.shape) for c in consts] + [tok(LANES)] * 3
                  + [_const_spec(c.shape) for c in gains] + [mem_spec, mem_spec]),
        out_specs=[tok(MLA_HEADS * LANES)] * 2 + [tok(BRANCH_W)] * 6,
        out_shape=[wide, wide] + [narrow] * 6,
        compiler_params=_params("arbitrary"),
        name="proj",
    )(x2, *consts, *tabs, *gains, km, vm)


def _chunk_mask(shape):
    row = lax.broadcasted_iota(jnp.int32, shape, 0)
    col = lax.broadcasted_iota(jnp.int32, shape, 1)
    return (col // CHUNK) <= (row // CHUNK)


def _online_update(s, m_ref, l_ref):
    m_old = m_ref[...]
    m_new = jnp.maximum(m_old, jnp.max(s, axis=-1, keepdims=True))
    alpha = jnp.exp(m_old - m_new)
    p = jnp.exp(s - m_new[:, :1])
    l_ref[...] = alpha * l_ref[...] + jnp.sum(p, axis=-1, keepdims=True)
    m_ref[...] = m_new
    return p, alpha


def _mla_attn_kernel(q_ref, k_ref, v_ref, o_ref, m_ref, l_ref, acc_ref):
    qi, ki = pl.program_id(1), pl.program_id(2)

    @pl.when(ki == 0)
    def _():
        m_ref[...] = jnp.full(m_ref.shape, NEG, jnp.float32)
        l_ref[...] = jnp.zeros(l_ref.shape, jnp.float32)
        acc_ref[...] = jnp.zeros(acc_ref.shape, jnp.float32)

    def step(masked):
        mask = _chunk_mask((q_ref.shape[0], k_ref.shape[0])) if masked else None
        for hd in range(MLA_HEADS):
            sl = slice(hd * LANES, (hd + 1) * LANES)
            s = _dot_nt(q_ref[:, sl], k_ref[:, sl])
            if masked:
                s = jnp.where(mask, s, NEG)
            p, alpha = _online_update(s, m_ref.at[hd], l_ref.at[hd])
            pair = slice((hd // 2) * LANES, (hd // 2 + 1) * LANES)
            acc_ref[hd] = alpha * acc_ref[hd] + _dot(_bf16(p), v_ref[:, pair])

    @pl.when(ki < qi)
    def _():
        step(False)

    @pl.when(ki == qi)
    def _():
        step(True)
        lane = lax.broadcasted_iota(jnp.int32, (q_ref.shape[0], LANES), 1)
        lo = lane < MLA_V
        for pr in range(MLA_HEADS // 2):
            even = acc_ref[2 * pr] / l_ref[2 * pr]
            odd = acc_ref[2 * pr + 1] / l_ref[2 * pr + 1]
            o_ref[:, pr * LANES:(pr + 1) * LANES] = _bf16(jnp.where(lo, even, odd))


def _mla_attn(q, k, v, batch, seq):
    ta = ATTN_TILE
    nb = seq // ta
    q_map = lambda b, qi, ki: (b * nb + qi, 0)
    kv_map = lambda b, qi, ki: (b * nb + jnp.minimum(ki, qi), 0)
    stat = pltpu.VMEM((MLA_HEADS, ta, LANES), jnp.float32)
    return pl.pallas_call(
        _mla_attn_kernel,
        grid=(batch, nb, nb),
        in_specs=[pl.BlockSpec((ta, MLA_HEADS * LANES), q_map),
                  pl.BlockSpec((ta, MLA_HEADS * LANES), kv_map),
                  pl.BlockSpec((ta, BRANCH_W), kv_map)],
        out_specs=pl.BlockSpec((ta, BRANCH_W), q_map),
        out_shape=jax.ShapeDtypeStruct((batch * seq, BRANCH_W), jnp.bfloat16),
        scratch_shapes=[stat, stat, stat],
        compiler_params=_params("arbitrary", "arbitrary", "arbitrary"),
        name="mla_attn",
    )(q, k, v)


def _diff_attn_kernel(lam_ref, q0_ref, q1_ref, k_ref, v_ref, bias_ref, gout_ref, o_ref,
                      m_ref, l_ref, acc_ref, *, out_scale):
    qi, ki = pl.program_id(1), pl.program_id(2)

    @pl.when(ki == 0)
    def _():
        m_ref[...] = jnp.full(m_ref.shape, NEG, jnp.float32)
        l_ref[...] = jnp.zeros(l_ref.shape, jnp.float32)
        acc_ref[...] = jnp.zeros(acc_ref.shape, jnp.float32)

    def step(bias_tile, masked):
        mask = _chunk_mask((q0_ref.shape[0], k_ref.shape[0])) if masked else None
        for hd in range(DIFF_HEADS):
            sl = slice(hd * LANES, (hd + 1) * LANES)
            kh = k_ref[:, sl]
            vh = v_ref[:, sl]
            for c, q_ref in enumerate((q0_ref, q1_ref)):
                s = _dot_nt(q_ref[:, sl], kh)
                if bias_tile is not None:
                    s = s + bias_ref[bias_tile, hd]
                if masked:
                    s = jnp.where(mask, s, NEG)
                idx = 2 * hd + c
                p, alpha = _online_update(s, m_ref.at[idx], l_ref.at[idx])
                acc_ref[idx] = alpha * acc_ref[idx] + _dot(_bf16(p), vh)

    @pl.when(ki < qi - 1)
    def _():
        step(None, False)

    @pl.when(ki == qi - 1)
    def _():
        step(1, False)

    @pl.when(ki == qi)
    def _():
        step(0, True)
        lam = lam_ref[0]
        g = gout_ref[...] * out_scale
        for hd in range(DIFF_HEADS):
            o = (acc_ref[2 * hd] / l_ref[2 * hd]
                 - lam * (acc_ref[2 * hd + 1] / l_ref[2 * hd + 1]))
            o_ref[:, hd * LANES:(hd + 1) * LANES] = _bf16(o * _rms_scale(o, DIFF_V) * g)


def _diff_attn(lam, q0, q1, k, v, bias, g_out, out_scale, batch, seq):
    ta = ATTN_TILE
    nb = seq // ta
    q_map = lambda b, qi, ki: (b * nb + qi, 0)
    kv_map = lambda b, qi, ki: (b * nb + jnp.minimum(ki, qi), 0)
    stat = pltpu.VMEM((2 * DIFF_HEADS, ta, LANES), jnp.float32)
    blk = lambda imap: pl.BlockSpec((ta, BRANCH_W), imap)
    return pl.pallas_call(
        functools.partial(_diff_attn_kernel, out_scale=out_scale),
        grid=(batch, nb, nb),
        in_specs=[pl.BlockSpec(memory_space=pltpu.SMEM),
                  blk(q_map), blk(q_map), blk(kv_map), blk(kv_map),
                  _const_spec(bias.shape), _const_spec(g_out.shape)],
        out_specs=blk(q_map),
        out_shape=jax.ShapeDtypeStruct((batch * seq, BRANCH_W), jnp.bfloat16),
        scratch_shapes=[stat, stat, stat],
        compiler_params=_params("arbitrary", "arbitrary", "arbitrary"),
        name="diff_attn",
    )(lam, q0, q1, k, v, bias, g_out)


def _merge_mlp_kernel(x_ref, oa_ref, ob_ref, oc_ref, gmix_ref, wg_ref, wb_ref, wout_ref,
                      gmlp_ref, w1_ref, w2_ref, out_ref):
    x = x_ref[...]
    h = _bf16(x * _rms_scale(x, D_MODEL) * gmix_ref[...])
    y = None
    for n, o_ref in enumerate((oa_ref, ob_ref, oc_ref)):
        gate = 1.0 / (1.0 + jnp.exp(-_dot(h, wg_ref[:, n * D_MODEL:(n + 1) * D_MODEL])))
        term = gate * _dot(o_ref[...], wb_ref[n])
        y = term if y is None else y + term
    x = x + _dot(_bf16(y), wout_ref[...])
    h2 = _bf16(x * _rms_scale(x, D_MODEL) * gmlp_ref[...])
    for c in range(D_FF // FF_CHUNK):
        f = jnp.maximum(_dot(h2, w1_ref[:, c * FF_CHUNK:(c + 1) * FF_CHUNK]), 0.0)
        x = x + _dot(_bf16(f * f), w2_ref[c * FF_CHUNK:(c + 1) * FF_CHUNK, :])
    out_ref[...] = x


def _merge_mlp(x2, o_a, o_b, o_c, w):
    t = x2.shape[0]
    tm = TOKEN_TILE
    row = lambda i: (i, 0)
    tok = lambda c: pl.BlockSpec((tm, c), row)
    consts = [w["g_mix"], w["w_gate"], w["w_branch"], w["w_out"], w["g_mlp"], w["w_ff1"], w["w_ff2"]]
    single = lambda c: pl.BlockSpec(c.shape, lambda *_: (0,) * c.ndim, pipeline_mode=pl.Buffered(1))
    return pl.pallas_call(
        _merge_mlp_kernel,
        grid=(t // tm,),
        in_specs=[tok(D_MODEL)] + [tok(BRANCH_W)] * 3 + [single(c) for c in consts],
        out_specs=tok(D_MODEL),
        out_shape=jax.ShapeDtypeStruct((t, D_MODEL), jnp.float32),
        compiler_params=_params("arbitrary"),
        name="merge_mlp",
    )(x2, o_a, o_b, o_c, *consts)


def _pad_heads(w, heads, width):
    w = w.reshape(w.shape[:-1] + (heads, width))
    pad = [(0, 0)] * (w.ndim - 1) + [(0, LANES - width)]
    return jnp.pad(w, pad).reshape(w.shape[:-2] + (heads * LANES,))


def _pad_gain(g, scale):
    return jnp.pad(g * scale, ((0, 0), (0, LANES - g.shape[-1])))[:, None, :]


def _layer_weights(w_in, g_mix, g_cq, w_uq, g_ckv, w_ukv, g_mla_q, g_mla_k, g_diff_q, g_diff_k,
                   g_mem_q, w_branch, w_out, g_mlp, w_ff1, w_ff2):
    splits = np.cumsum([Q_LORA, KV_LORA, MLA_ROPE, DIFF_HEADS * 2 * DIFF_QK, DIFF_HEADS * 2 * DIFF_QK,
                        DIFF_HEADS * DIFF_V, MEM_HEADS * MEM_HEAD_DIM]).tolist()
    c_kr, c_dq, c_gate = splits[1], splits[2], splits[6]
    w_kr = jnp.pad(w_in[:, :, c_kr:c_dq], ((0, 0), (0, 0), (ROPE_LO, LANES - ROPE_HI)))
    w_cat = jnp.concatenate([w_in[:, :, :c_kr], w_kr, w_in[:, :, c_dq:c_gate]], axis=-1)
    assert w_cat.shape[-1] == _C_END
    w_ukv = w_ukv.reshape(DEPTH, KV_LORA, MLA_HEADS, MLA_NOPE + MLA_V)
    w_uk = _pad_heads(w_ukv[..., :MLA_NOPE].reshape(DEPTH, KV_LORA, -1), MLA_HEADS, MLA_NOPE)
    w_uv = w_ukv[..., MLA_NOPE:].reshape(DEPTH, KV_LORA, -1)
    vec = lambda g: g[:, None, :]
    return {
        "g_mix": vec(g_mix), "w_in": _bf16(w_cat), "g_cq": vec(g_cq),
        "w_uq": _bf16(_pad_heads(w_uq, MLA_HEADS, MLA_QK)), "g_ckv": vec(g_ckv),
        "w_uk": _bf16(w_uk), "w_uv": _bf16(w_uv),
        "g_q": _pad_gain(g_mla_q, MLA_QK ** -0.5), "g_k": _pad_gain(g_mla_k, 1.0),
        "g_dq": vec(jnp.tile(g_diff_q, (1, 2)) * DIFF_QK ** -0.5), "g_dk": vec(jnp.tile(g_diff_k, (1, 2))),
        "g_mq": vec(g_mem_q * MEM_HEAD_DIM ** -0.5),
        "w_gate": _bf16(w_in[:, :, c_gate:]), "w_branch": _bf16(w_branch), "w_out": _bf16(w_out),
        "g_mlp": vec(g_mlp), "w_ff1": _bf16(w_ff1), "w_ff2": _bf16(w_ff2),
    }


def kernel(x, mem, positions, t5_table, g_mix, g_mem, w_in, g_cq, w_uq, g_ckv, w_ukv, g_mla_q, g_mla_k, g_diff_q, g_diff_k, lam_q1, lam_k1, lam_q2, lam_k2, g_diff_out, w_mem_kv, g_mem_q, g_mem_k, w_branch, w_out, g_mlp, w_ff1, w_ff2):
    b, s, _ = x.shape
    assert s % ATTN_TILE == 0 and s % TOKEN_TILE == 0 and (b * s) % TABLE_TILE == 0
    tabs = _rope_tables(positions)
    bias = _t5_bias_tiles(t5_table)
    km, vm = _mem_kv(mem, g_mem, w_mem_kv, g_mem_k)
    weights = _layer_weights(w_in, g_mix, g_cq, w_uq, g_ckv, w_ukv, g_mla_q, g_mla_k, g_diff_q,
                             g_diff_k, g_mem_q, w_branch, w_out, g_mlp, w_ff1, w_ff2)
    f32 = jnp.float32
    lam_init = [0.8 - 0.6 * math.exp(-0.3 * l) for l in range(DEPTH)]
    lam = (jnp.exp(jnp.sum(lam_q1.astype(f32) * lam_k1.astype(f32), axis=-1))
           - jnp.exp(jnp.sum(lam_q2.astype(f32) * lam_k2.astype(f32), axis=-1))
           + jnp.asarray(lam_init, f32))
    x2 = x.reshape(b * s, D_MODEL)
    for l in range(DEPTH):
        w = {name: v[l] for name, v in weights.items()}
        q, k, v, dq0, dq1, dk, dv, o_c = _proj(l, x2, w, tabs, km, vm, s)
        o_a = _mla_attn(q, k, v, b, s)
        o_b = _diff_attn(lam[l:l + 1], dq0, dq1, dk, dv, bias, g_diff_out[l][None, :],
                         1.0 - lam_init[l], b, s)
        x2 = _merge_mlp(x2, o_a, o_b, o_c, w)
    return x2.reshape(b, s, D_MODEL)
```

```python
import functools
import math

import jax
import jax.numpy as jnp
import numpy as np
from jax import lax
from jax.experimental import pallas as pl
from jax.experimental.pallas import tpu as pltpu

D_MODEL = 1024
DEPTH = 4
CHUNK = 64
MLA_HEADS = 8
MLA_NOPE = 64
MLA_ROPE = 32
MLA_V = 64
MLA_QK = MLA_NOPE + MLA_ROPE
Q_LORA = 384
KV_LORA = 256
ROPE_BASE = 10000.0
DIFF_HEADS = 4
DIFF_QK = 64
DIFF_V = 2 * DIFF_QK
MEM_HEADS = 4
MEM_HEAD_DIM = 128
BRANCH_W = 512
N_BRANCH = 3
D_FF = 4 * D_MODEL
T5_BUCKETS = 32
T5_MAX_DIST = 128
EPS = 1e-6
NEG = -1e30

LANES = 128
V7X_VMEM_BYTES = 64 * 1024 * 1024
VMEM_LIMIT = V7X_VMEM_BYTES * 7 // 8

TOKEN_TILE = 512
ATTN_TILE = 512
TABLE_TILE = 2048
FF_CHUNK = 1024

ROPE_LO = MLA_NOPE
ROPE_MID = MLA_NOPE + MLA_ROPE // 2
ROPE_HI = MLA_NOPE + MLA_ROPE

_C_CQ = 0
_C_CKV = _C_CQ + Q_LORA
_C_KR = _C_CKV + KV_LORA
_C_DQ = _C_KR + LANES
_C_DK = _C_DQ + DIFF_HEADS * LANES
_C_MQ = _C_DK + DIFF_HEADS * LANES
_C_END = _C_MQ + MEM_HEADS * MEM_HEAD_DIM

BF16_SUBLANES = 16
MLA_VROWS = -(-(MLA_V + 1) // BF16_SUBLANES) * BF16_SUBLANES
DIFF_VROWS = -(-(DIFF_V + 1) // BF16_SUBLANES) * BF16_SUBLANES
LOG2E = math.log2(math.e)


def _bf16(x):
    return x.astype(jnp.bfloat16)


def _dot(a, b):
    return jnp.dot(a, b, preferred_element_type=jnp.float32)


def _dot_nt(a, b):
    return lax.dot_general(a, b, (((1,), (1,)), ((), ())), preferred_element_type=jnp.float32)


def _rms_scale(x, n):
    return lax.rsqrt(jnp.sum(x * x, axis=-1, keepdims=True) * (1.0 / n) + EPS)


def _const_spec(shape):
    return pl.BlockSpec(shape, lambda *_: (0,) * len(shape))


def _params(*sem):
    return pltpu.CompilerParams(dimension_semantics=sem, vmem_limit_bytes=VMEM_LIMIT)


def _rope_tables_kernel(pos_ref, inv_ref, cos_ref, sina_ref, sinb_ref):
    ang = pos_ref[...].astype(jnp.float32) * inv_ref[...]
    lane = lax.broadcasted_iota(jnp.int32, ang.shape, 1)
    first = (lane >= ROPE_LO) & (lane < ROPE_MID)
    second = (lane >= ROPE_MID) & (lane < ROPE_HI)
    c = jnp.cos(ang)
    s = jnp.sin(ang)
    cos_ref[...] = jnp.where(first | second, c, 1.0)
    sina_ref[...] = jnp.where(second, s, 0.0)
    sinb_ref[...] = jnp.where(first, -s, 0.0)


def _rope_tables(positions):
    t = positions.size
    half = MLA_ROPE // 2
    inv = jnp.power(jnp.float32(ROPE_BASE), -jnp.arange(half, dtype=jnp.float32) / half)
    inv_full = jnp.zeros((1, LANES), jnp.float32)
    inv_full = inv_full.at[0, ROPE_LO:ROPE_MID].set(inv).at[0, ROPE_MID:ROPE_HI].set(inv)
    out = jax.ShapeDtypeStruct((t, LANES), jnp.float32)
    return pl.pallas_call(
        _rope_tables_kernel,
        grid=(t // TABLE_TILE,),
        in_specs=[pl.BlockSpec((TABLE_TILE, 1), lambda i: (i, 0)), _const_spec((1, LANES))],
        out_specs=[pl.BlockSpec((TABLE_TILE, LANES), lambda i: (i, 0))] * 3,
        out_shape=[out] * 3,
        compiler_params=_params("arbitrary"),
        name="rope_tables",
    )(positions.reshape(t, 1), inv_full)


def _t5_large_thresholds():
    n = T5_BUCKETS // 2
    max_exact = n // 2
    assert T5_MAX_DIST == 16 * max_exact and n - max_exact == 8
    out = []
    for j in range(1, n - max_exact):
        a = max_exact
        while a * a < max_exact * max_exact * 2 ** j:
            a += 1
        out.append(a)
    return out


def _t5_bias_kernel(table_ref, bias_ref):
    t = bias_ref.shape[-1]
    off = pl.program_id(0) * t
    row = lax.broadcasted_iota(jnp.int32, (t, t), 0)
    col = lax.broadcasted_iota(jnp.int32, (t, t), 1)
    rel = row - col - off
    n = T5_BUCKETS // 2
    max_exact = n // 2
    a = jnp.abs(rel)
    large = jnp.full((t, t), max_exact, jnp.int32)
    for thr in _t5_large_thresholds():
        large = large + (a >= thr).astype(jnp.int32)
    bucket = jnp.where(rel > 0, n, 0) + jnp.where(a < max_exact, a, large)
    far_bucket = n - 1
    for h in range(DIFF_HEADS):
        val = jnp.full((t, t), table_ref[0, h], jnp.float32)
        for b in range(1, T5_BUCKETS):
            val = jnp.where(bucket == b, table_ref[b, h], val)
        bias_ref[0, h] = (val - table_ref[far_bucket, h]) * LOG2E


def _t5_bias_tiles(t5_table):
    t = ATTN_TILE
    assert t >= _t5_large_thresholds()[-1]
    return pl.pallas_call(
        _t5_bias_kernel,
        grid=(2,),
        in_specs=[pl.BlockSpec(memory_space=pltpu.SMEM)],
        out_specs=pl.BlockSpec((1, DIFF_HEADS, t, t), lambda i: (i, 0, 0, 0)),
        out_shape=jax.ShapeDtypeStruct((2, DIFF_HEADS, t, t), jnp.float32),
        compiler_params=_params("arbitrary"),
        name="t5_bias_tiles",
    )(t5_table.astype(jnp.float32))


def _mem_kv_kernel(mem_ref, g_ref, w_ref, gk_ref, km_ref, vm_ref):
    x = mem_ref[0]
    h = _bf16(x * _rms_scale(x, D_MODEL) * g_ref[0])
    kv = _dot(h, w_ref[0])
    width = MEM_HEADS * MEM_HEAD_DIM
    for hd in range(MEM_HEADS):
        sl = slice(hd * MEM_HEAD_DIM, (hd + 1) * MEM_HEAD_DIM)
        k = kv[:, sl]
        km_ref[0, 0, :, sl] = _bf16(k * _rms_scale(k, MEM_HEAD_DIM) * gk_ref[0])
    vm_ref[0, 0] = _bf16(kv[:, width:])


def _mem_kv(mem, g_mem, w_mem_kv, g_mem_k):
    b, m, _ = mem.shape
    width = MEM_HEADS * MEM_HEAD_DIM
    out = jax.ShapeDtypeStruct((DEPTH, b, m, width), jnp.bfloat16)
    return pl.pallas_call(
        _mem_kv_kernel,
        grid=(DEPTH, b),
        in_specs=[
            pl.BlockSpec((1, m, D_MODEL), lambda l, i: (i, 0, 0)),
            pl.BlockSpec((1, 1, D_MODEL), lambda l, i: (l, 0, 0)),
            pl.BlockSpec((1, D_MODEL, 2 * width), lambda l, i: (l, 0, 0)),
            pl.BlockSpec((1, 1, MEM_HEAD_DIM), lambda l, i: (l, 0, 0)),
        ],
        out_specs=[pl.BlockSpec((1, 1, m, width), lambda l, i: (l, i, 0, 0))] * 2,
        out_shape=[out] * 2,
        compiler_params=_params("arbitrary", "arbitrary"),
        name="mem_kv",
    )(mem, g_mem.reshape(DEPTH, 1, D_MODEL), _bf16(w_mem_kv), g_mem_k.reshape(DEPTH, 1, MEM_HEAD_DIM))


def _rope(x, cos_t, sina_t, sinb_t):
    half = MLA_ROPE // 2
    return x * cos_t + pltpu.roll(x, half, 1) * sina_t + pltpu.roll(x, LANES - half, 1) * sinb_t


def _proj_kernel(x_ref, gmix_ref, win_ref, gcq_ref, wuq_ref, gckv_ref, wuk_ref, wuvt_ref, vones_ref,
                 wdvt_ref, dvones_ref, gq_ref, gk_ref, cos_ref, sina_ref, sinb_ref, gdq_ref, gdk_ref,
                 gmq_ref, km_ref, vm_ref,
                 q_ref, k_ref, vt_ref, dq0_ref, dq1_ref, dk_ref, dvt_ref, oc_ref):
    x = x_ref[...]
    h = _bf16(x * _rms_scale(x, D_MODEL) * gmix_ref[...])
    z = _dot(h, win_ref[...])
    cos_t, sina_t, sinb_t = cos_ref[...], sina_ref[...], sinb_ref[...]

    c_q = z[:, _C_CQ:_C_CKV]
    q_raw = _dot(_bf16(c_q * _rms_scale(c_q, Q_LORA) * gcq_ref[...]), wuq_ref[...])
    c_kv = z[:, _C_CKV:_C_KR]
    ckv_n = _bf16(c_kv * _rms_scale(c_kv, KV_LORA) * gckv_ref[...])
    k_nope = _dot(ckv_n, wuk_ref[...])
    vt_ref[...] = _bf16(_dot_nt(wuvt_ref[...], ckv_n) + vones_ref[...])
    k_rope = z[:, _C_KR:_C_DQ]
    gq, gk = gq_ref[...], gk_ref[...]
    for hd in range(MLA_HEADS):
        sl = slice(hd * LANES, (hd + 1) * LANES)
        q = q_raw[:, sl]
        q_ref[:, sl] = _bf16(_rope(q * _rms_scale(q, MLA_QK) * gq, cos_t, sina_t, sinb_t))
        k = k_nope[:, sl] + k_rope
        k_ref[:, sl] = _bf16(_rope(k * _rms_scale(k, MLA_QK) * gk, cos_t, sina_t, sinb_t))

    lane = lax.broadcasted_iota(jnp.int32, (x.shape[0], LANES), 1)
    lo = lane < DIFF_QK
    gdq, gdk = gdq_ref[...], gdk_ref[...]

    def half_norm(t, g):
        t2 = t * t
        s_lo = jnp.sum(jnp.where(lo, t2, 0.0), axis=-1, keepdims=True)
        s_hi = jnp.sum(jnp.where(lo, 0.0, t2), axis=-1, keepdims=True)
        ms = jnp.where(lo, s_lo, s_hi) * (1.0 / DIFF_QK)
        return t * lax.rsqrt(ms + EPS) * g

    for hd in range(DIFF_HEADS):
        sl = slice(hd * LANES, (hd + 1) * LANES)
        dq = half_norm(z[:, _C_DQ + hd * LANES:_C_DQ + (hd + 1) * LANES], gdq)
        dq0_ref[:, sl] = _bf16(jnp.where(lo, dq, 0.0))
        dq1_ref[:, sl] = _bf16(jnp.where(lo, 0.0, dq))
        dk_ref[:, sl] = _bf16(half_norm(z[:, _C_DK + hd * LANES:_C_DK + (hd + 1) * LANES], gdk))
    dvt_ref[...] = _bf16(_dot_nt(wdvt_ref[...], h) + dvones_ref[...])

    gmq = gmq_ref[...]
    for hd in range(MEM_HEADS):
        sl = slice(hd * MEM_HEAD_DIM, (hd + 1) * MEM_HEAD_DIM)
        mq = z[:, _C_MQ + hd * MEM_HEAD_DIM:_C_MQ + (hd + 1) * MEM_HEAD_DIM]
        mq = _bf16(mq * _rms_scale(mq, MEM_HEAD_DIM) * gmq)
        s = _dot_nt(mq, km_ref[0, 0, :, sl])
        p = jnp.exp(s - jnp.max(s, axis=-1, keepdims=True))
        o = _dot(_bf16(p), vm_ref[0, 0, :, sl])
        oc_ref[:, sl] = _bf16(o / jnp.sum(p, axis=-1, keepdims=True))


def _proj(layer, x2, w, tabs, km, vm, seq):
    t = x2.shape[0]
    tm = TOKEN_TILE
    steps_per_batch = seq // tm
    m = km.shape[2]
    width = MEM_HEADS * MEM_HEAD_DIM
    row = lambda i: (i, 0)
    tok = lambda c: pl.BlockSpec((tm, c), row)
    mem_spec = pl.BlockSpec((1, 1, m, width), lambda i: (layer, i // steps_per_batch, 0, 0))
    wide = jax.ShapeDtypeStruct((t, MLA_HEADS * LANES), jnp.bfloat16)
    narrow = jax.ShapeDtypeStruct((t, BRANCH_W), jnp.bfloat16)
    col = lambda i: (0, i)
    vt_rows, dvt_rows = MLA_HEADS * MLA_VROWS, DIFF_HEADS * DIFF_VROWS
    consts = [w["g_mix"], w["w_in"], w["g_cq"], w["w_uq"], w["g_ckv"], w["w_uk"], w["w_uvt"],
              _ones_rows(MLA_HEADS, MLA_VROWS, MLA_V), w["w_dvt"],
              _ones_rows(DIFF_HEADS, DIFF_VROWS, DIFF_V), w["g_q"], w["g_k"]]
    gains = [w["g_dq"], w["g_dk"], w["g_mq"]]
    return pl.pallas_call(
        _proj_kernel,
        grid=(t // tm,),
        in_specs=([tok(D_MODEL)] + [_const_spec(c.shape) for c in consts] + [tok(LANES)] * 3
                  + [_const_spec(c.shape) for c in gains] + [mem_spec, mem_spec]),
        out_specs=([tok(MLA_HEADS * LANES)] * 2 + [pl.BlockSpec((vt_rows, tm), col)] + [tok(BRANCH_W)] * 3
                   + [pl.BlockSpec((dvt_rows, tm), col), tok(BRANCH_W)]),
        out_shape=([wide, wide, jax.ShapeDtypeStruct((vt_rows, t), jnp.bfloat16)] + [narrow] * 3
                   + [jax.ShapeDtypeStruct((dvt_rows, t), jnp.bfloat16), narrow]),
        compiler_params=_params("arbitrary"),
        name="proj",
    )(x2, *consts, *tabs, *gains, km, vm)


def _chunk_mask_t(shape):
    krow = lax.broadcasted_iota(jnp.int32, shape, 0)
    qcol = lax.broadcasted_iota(jnp.int32, shape, 1)
    return (krow // CHUNK) <= (qcol // CHUNK)


def _flash_step(n, score_fn, vt_fn, m_ref, acc_ref):
    def scores_and_max(i):
        s = score_fn(i)
        m_old = m_ref[i]
        m_new = jnp.maximum(m_old, jnp.max(s, axis=0, keepdims=True))
        m_ref[i] = m_new
        return s, m_new, jnp.exp2(m_old - m_new)

    def exponentials(a):
        s, m_new, alpha = a
        return _bf16(jnp.exp2(s - m_new)), alpha

    def value_product(i, b):
        p, alpha = b
        acc_ref[i] = alpha * acc_ref[i] + _dot(vt_fn(i), p)

    stage_a, stage_b = {}, {}
    for it in range(n + 2):
        if it < n:
            stage_a[it] = scores_and_max(it)
        if 0 <= it - 1 < n:
            stage_b[it - 1] = exponentials(stage_a.pop(it - 1))
        if 0 <= it - 2 < n:
            value_product(it - 2, stage_b.pop(it - 2))


def _init_flash(m_ref, acc_ref):
    m_ref[...] = jnp.full(m_ref.shape, NEG, jnp.float32)
    acc_ref[...] = jnp.zeros(acc_ref.shape, jnp.float32)


def _mla_attn_kernel(q_ref, k_ref, vt_ref, o_ref, m_ref, acc_ref):
    qi, ki = pl.program_id(1), pl.program_id(2)

    @pl.when(ki == 0)
    def _():
        _init_flash(m_ref, acc_ref)

    def step(masked):
        mask = _chunk_mask_t((k_ref.shape[0], q_ref.shape[0])) if masked else None

        def score(hd):
            sl = slice(hd * LANES, (hd + 1) * LANES)
            s = _dot_nt(k_ref[:, sl], q_ref[:, sl])
            return jnp.where(mask, s, NEG) if masked else s

        _flash_step(MLA_HEADS, score, lambda hd: vt_ref[hd * MLA_VROWS:(hd + 1) * MLA_VROWS, :],
                    m_ref, acc_ref)

    @pl.when(ki < qi)
    def _():
        step(False)

    @pl.when(ki == qi)
    def _():
        step(True)
        outs = []
        for hd in range(MLA_HEADS):
            a = acc_ref[hd]
            outs.append(a[:MLA_V] / a[MLA_V:MLA_V + 1])
        o_ref[...] = _bf16(jnp.concatenate(outs, axis=0).T)


def _mla_attn(q, k, vt, batch, seq):
    ta = ATTN_TILE
    nb = seq // ta
    q_map = lambda b, qi, ki: (b * nb + qi, 0)
    kv_map = lambda b, qi, ki: (b * nb + jnp.minimum(ki, qi), 0)
    vt_map = lambda b, qi, ki: (0, b * nb + jnp.minimum(ki, qi))
    return pl.pallas_call(
        _mla_attn_kernel,
        grid=(batch, nb, nb),
        in_specs=[pl.BlockSpec((ta, MLA_HEADS * LANES), q_map),
                  pl.BlockSpec((ta, MLA_HEADS * LANES), kv_map),
                  pl.BlockSpec((MLA_HEADS * MLA_VROWS, ta), vt_map)],
        out_specs=pl.BlockSpec((ta, BRANCH_W), q_map),
        out_shape=jax.ShapeDtypeStruct((batch * seq, BRANCH_W), jnp.bfloat16),
        scratch_shapes=[pltpu.VMEM((MLA_HEADS, 1, ta), jnp.float32),
                        pltpu.VMEM((MLA_HEADS, MLA_VROWS, ta), jnp.float32)],
        compiler_params=_params("arbitrary", "arbitrary", "arbitrary"),
        name="mla_attn",
    )(q, k, vt)


def _diff_attn_kernel(lam_ref, q0_ref, q1_ref, k_ref, vt_ref, bias_ref, gout_ref, o_ref,
                      m_ref, acc_ref, *, out_scale):
    qi, ki = pl.program_id(1), pl.program_id(2)

    @pl.when(ki == 0)
    def _():
        _init_flash(m_ref, acc_ref)

    def step(bias_tile, masked):
        mask = _chunk_mask_t((k_ref.shape[0], q0_ref.shape[0])) if masked else None

        def score(idx):
            hd, q_ref = idx // 2, (q0_ref, q1_ref)[idx % 2]
            sl = slice(hd * LANES, (hd + 1) * LANES)
            s = _dot_nt(k_ref[:, sl], q_ref[:, sl])
            if bias_tile is not None:
                s = s + bias_ref[bias_tile, hd]
            return jnp.where(mask, s, NEG) if masked else s

        _flash_step(2 * DIFF_HEADS, score,
                    lambda idx: vt_ref[(idx // 2) * DIFF_VROWS:(idx // 2 + 1) * DIFF_VROWS, :],
                    m_ref, acc_ref)

    @pl.when(ki < qi - 1)
    def _():
        step(None, False)

    @pl.when(ki == qi - 1)
    def _():
        step(1, False)

    @pl.when(ki == qi)
    def _():
        step(0, True)
        lam = lam_ref[0]
        g = gout_ref[...] * out_scale
        outs = []
        for hd in range(DIFF_HEADS):
            a0, a1 = acc_ref[2 * hd], acc_ref[2 * hd + 1]
            o = a0[:DIFF_V] / a0[DIFF_V:DIFF_V + 1] - lam * (a1[:DIFF_V] / a1[DIFF_V:DIFF_V + 1])
            ms = jnp.sum(o * o, axis=0, keepdims=True) * (1.0 / DIFF_V)
            outs.append(o * lax.rsqrt(ms + EPS) * g)
        o_ref[...] = _bf16(jnp.concatenate(outs, axis=0).T)


def _diff_attn(lam, q0, q1, k, vt, bias, g_out, out_scale, batch, seq):
    ta = ATTN_TILE
    nb = seq // ta
    q_map = lambda b, qi, ki: (b * nb + qi, 0)
    kv_map = lambda b, qi, ki: (b * nb + jnp.minimum(ki, qi), 0)
    vt_map = lambda b, qi, ki: (0, b * nb + jnp.minimum(ki, qi))
    blk = lambda imap: pl.BlockSpec((ta, BRANCH_W), imap)
    return pl.pallas_call(
        functools.partial(_diff_attn_kernel, out_scale=out_scale),
        grid=(batch, nb, nb),
        in_specs=[pl.BlockSpec(memory_space=pltpu.SMEM),
                  blk(q_map), blk(q_map), blk(kv_map),
                  pl.BlockSpec((DIFF_HEADS * DIFF_VROWS, ta), vt_map),
                  _const_spec(bias.shape), _const_spec(g_out.shape)],
        out_specs=blk(q_map),
        out_shape=jax.ShapeDtypeStruct((batch * seq, BRANCH_W), jnp.bfloat16),
        scratch_shapes=[pltpu.VMEM((2 * DIFF_HEADS, 1, ta), jnp.float32),
                        pltpu.VMEM((2 * DIFF_HEADS, DIFF_VROWS, ta), jnp.float32)],
        compiler_params=_params("arbitrary", "arbitrary", "arbitrary"),
        name="diff_attn",
    )(lam, q0, q1, k, vt, bias, g_out)


def _merge_mlp_kernel(x_ref, oa_ref, ob_ref, oc_ref, gmix_ref, wg_ref, wb_ref, wout_ref,
                      gmlp_ref, w1_ref, w2_ref, out_ref):
    x = x_ref[...]
    h = _bf16(x * _rms_scale(x, D_MODEL) * gmix_ref[...])
    y = None
    for n, o_ref in enumerate((oa_ref, ob_ref, oc_ref)):
        gate = 1.0 / (1.0 + jnp.exp(-_dot(h, wg_ref[:, n * D_MODEL:(n + 1) * D_MODEL])))
        term = gate * _dot(o_ref[...], wb_ref[n])
        y = term if y is None else y + term
    x = x + _dot(_bf16(y), wout_ref[...])
    h2 = _bf16(x * _rms_scale(x, D_MODEL) * gmlp_ref[...])
    for c in range(D_FF // FF_CHUNK):
        f = jnp.maximum(_dot(h2, w1_ref[:, c * FF_CHUNK:(c + 1) * FF_CHUNK]), 0.0)
        x = x + _dot(_bf16(f * f), w2_ref[c * FF_CHUNK:(c + 1) * FF_CHUNK, :])
    out_ref[...] = x


def _merge_mlp(x2, o_a, o_b, o_c, w):
    t = x2.shape[0]
    tm = TOKEN_TILE
    row = lambda i: (i, 0)
    tok = lambda c: pl.BlockSpec((tm, c), row)
    consts = [w["g_mix"], w["w_gate"], w["w_branch"], w["w_out"], w["g_mlp"], w["w_ff1"], w["w_ff2"]]
    single = lambda c: pl.BlockSpec(c.shape, lambda *_: (0,) * c.ndim, pipeline_mode=pl.Buffered(1))
    return pl.pallas_call(
        _merge_mlp_kernel,
        grid=(t // tm,),
        in_specs=[tok(D_MODEL)] + [tok(BRANCH_W)] * 3 + [single(c) for c in consts],
        out_specs=tok(D_MODEL),
        out_shape=jax.ShapeDtypeStruct((t, D_MODEL), jnp.float32),
        compiler_params=_params("arbitrary"),
        name="merge_mlp",
    )(x2, o_a, o_b, o_c, *consts)


def _pad_heads(w, heads, width):
    w = w.reshape(w.shape[:-1] + (heads, width))
    pad = [(0, 0)] * (w.ndim - 1) + [(0, LANES - width)]
    return jnp.pad(w, pad).reshape(w.shape[:-2] + (heads * LANES,))


def _pad_gain(g, scale):
    return jnp.pad(g * scale, ((0, 0), (0, LANES - g.shape[-1])))[:, None, :]


def _transposed_value_weight(w, rows):
    depth, n_in, heads, width = w.shape
    wt = jnp.pad(jnp.transpose(w, (0, 2, 3, 1)), ((0, 0), (0, 0), (0, rows - width), (0, 0)))
    return wt.reshape(depth, heads * rows, n_in)


def _ones_rows(heads, rows, at):
    col = np.zeros((heads, rows, 1), np.float32)
    col[:, at] = 1.0
    return jnp.asarray(col.reshape(heads * rows, 1))


def _layer_weights(w_in, g_mix, g_cq, w_uq, g_ckv, w_ukv, g_mla_q, g_mla_k, g_diff_q, g_diff_k,
                   g_mem_q, w_branch, w_out, g_mlp, w_ff1, w_ff2):
    splits = np.cumsum([Q_LORA, KV_LORA, MLA_ROPE, DIFF_HEADS * 2 * DIFF_QK, DIFF_HEADS * 2 * DIFF_QK,
                        DIFF_HEADS * DIFF_V, MEM_HEADS * MEM_HEAD_DIM]).tolist()
    c_kr, c_dq, c_dv, c_mq, c_gate = splits[1], splits[2], splits[4], splits[5], splits[6]
    w_kr = jnp.pad(w_in[:, :, c_kr:c_dq], ((0, 0), (0, 0), (ROPE_LO, LANES - ROPE_HI)))
    w_cat = jnp.concatenate([w_in[:, :, :c_kr], w_kr, w_in[:, :, c_dq:c_dv], w_in[:, :, c_mq:c_gate]],
                            axis=-1)
    assert w_cat.shape[-1] == _C_END
    w_ukv = w_ukv.reshape(DEPTH, KV_LORA, MLA_HEADS, MLA_NOPE + MLA_V)
    w_uk = _pad_heads(w_ukv[..., :MLA_NOPE].reshape(DEPTH, KV_LORA, -1), MLA_HEADS, MLA_NOPE)
    w_uvt = _transposed_value_weight(w_ukv[..., MLA_NOPE:], MLA_VROWS)
    w_dvt = _transposed_value_weight(
        w_in[:, :, c_dv:c_mq].reshape(DEPTH, D_MODEL, DIFF_HEADS, DIFF_V), DIFF_VROWS)
    vec = lambda g: g[:, None, :]
    return {
        "g_mix": vec(g_mix), "w_in": _bf16(w_cat), "g_cq": vec(g_cq),
        "w_uq": _bf16(_pad_heads(w_uq, MLA_HEADS, MLA_QK)), "g_ckv": vec(g_ckv),
        "w_uk": _bf16(w_uk), "w_uvt": _bf16(w_uvt), "w_dvt": _bf16(w_dvt),
        "g_q": _pad_gain(g_mla_q, MLA_QK ** -0.5 * LOG2E), "g_k": _pad_gain(g_mla_k, 1.0),
        "g_dq": vec(jnp.tile(g_diff_q, (1, 2)) * (DIFF_QK ** -0.5 * LOG2E)),
        "g_dk": vec(jnp.tile(g_diff_k, (1, 2))),
        "g_mq": vec(g_mem_q * MEM_HEAD_DIM ** -0.5),
        "w_gate": _bf16(w_in[:, :, c_gate:]), "w_branch": _bf16(w_branch), "w_out": _bf16(w_out),
        "g_mlp": vec(g_mlp), "w_ff1": _bf16(w_ff1), "w_ff2": _bf16(w_ff2),
    }


def kernel(x, mem, positions, t5_table, g_mix, g_mem, w_in, g_cq, w_uq, g_ckv, w_ukv, g_mla_q, g_mla_k, g_diff_q, g_diff_k, lam_q1, lam_k1, lam_q2, lam_k2, g_diff_out, w_mem_kv, g_mem_q, g_mem_k, w_branch, w_out, g_mlp, w_ff1, w_ff2):
    b, s, _ = x.shape
    assert s % ATTN_TILE == 0 and s % TOKEN_TILE == 0 and (b * s) % TABLE_TILE == 0
    tabs = _rope_tables(positions)
    bias = _t5_bias_tiles(t5_table)
    km, vm = _mem_kv(mem, g_mem, w_mem_kv, g_mem_k)
    weights = _layer_weights(w_in, g_mix, g_cq, w_uq, g_ckv, w_ukv, g_mla_q, g_mla_k, g_diff_q,
                             g_diff_k, g_mem_q, w_branch, w_out, g_mlp, w_ff1, w_ff2)
    f32 = jnp.float32
    lam_init = [0.8 - 0.6 * math.exp(-0.3 * l) for l in range(DEPTH)]
    lam = (jnp.exp(jnp.sum(lam_q1.astype(f32) * lam_k1.astype(f32), axis=-1))
           - jnp.exp(jnp.sum(lam_q2.astype(f32) * lam_k2.astype(f32), axis=-1))
           + jnp.asarray(lam_init, f32))
    x2 = x.reshape(b * s, D_MODEL)
    for l in range(DEPTH):
        w = {name: v[l] for name, v in weights.items()}
        q, k, vt, dq0, dq1, dk, dvt, o_c = _proj(l, x2, w, tabs, km, vm, s)
        o_a = _mla_attn(q, k, vt, b, s)
        o_b = _diff_attn(lam[l:l + 1], dq0, dq1, dk, dvt, bias, g_diff_out[l][:, None],
                         1.0 - lam_init[l], b, s)
        x2 = _merge_mlp(x2, o_a, o_b, o_c, w)
    return x2.reshape(b, s, D_MODEL)
```

```python
import functools
import math

import jax
import jax.numpy as jnp
import numpy as np
from jax import lax
from jax.experimental import pallas as pl
from jax.experimental.pallas import tpu as pltpu

D_MODEL = 1024
DEPTH = 4
CHUNK = 64
MLA_HEADS = 8
MLA_NOPE = 64
MLA_ROPE = 32
MLA_V = 64
MLA_QK = MLA_NOPE + MLA_ROPE
Q_LORA = 384
KV_LORA = 256
ROPE_BASE = 10000.0
DIFF_HEADS = 4
DIFF_QK = 64
DIFF_V = 2 * DIFF_QK
MEM_HEADS = 4
MEM_HEAD_DIM = 128
BRANCH_W = 512
N_BRANCH = 3
D_FF = 4 * D_MODEL
T5_BUCKETS = 32
T5_MAX_DIST = 128
EPS = 1e-6
NEG = -1e30

LANES = 128
V7X_VMEM_BYTES = 64 * 1024 * 1024
VMEM_LIMIT = V7X_VMEM_BYTES * 7 // 8

TOKEN_TILE = 512
ATTN_TILE = 512
TABLE_TILE = 2048
FF_CHUNK = 1024

ROPE_LO = MLA_NOPE
ROPE_MID = MLA_NOPE + MLA_ROPE // 2
ROPE_HI = MLA_NOPE + MLA_ROPE

_C_CQ = 0
_C_CKV = _C_CQ + Q_LORA
_C_KR = _C_CKV + KV_LORA
_C_DQ = _C_KR + LANES
_C_DK = _C_DQ + DIFF_HEADS * LANES
_C_MQ = _C_DK + DIFF_HEADS * LANES
_C_END = _C_MQ + MEM_HEADS * MEM_HEAD_DIM

BF16_SUBLANES = 16
MLA_VROWS = -(-(MLA_V + 1) // BF16_SUBLANES) * BF16_SUBLANES
DIFF_VROWS = -(-(DIFF_V + 1) // BF16_SUBLANES) * BF16_SUBLANES
LOG2E = math.log2(math.e)


def _bf16(x):
    return x.astype(jnp.bfloat16)


def _dot(a, b):
    return jnp.dot(a, b, preferred_element_type=jnp.float32)


def _dot_nt(a, b):
    return lax.dot_general(a, b, (((1,), (1,)), ((), ())), preferred_element_type=jnp.float32)


def _rms_scale(x, n):
    return lax.rsqrt(jnp.sum(x * x, axis=-1, keepdims=True) * (1.0 / n) + EPS)


def _const_spec(shape):
    return pl.BlockSpec(shape, lambda *_: (0,) * len(shape))


def _params(*sem):
    return pltpu.CompilerParams(dimension_semantics=sem, vmem_limit_bytes=VMEM_LIMIT)


def _rope_tables_kernel(pos_ref, inv_ref, cos_ref, sina_ref, sinb_ref):
    ang = pos_ref[...].astype(jnp.float32) * inv_ref[...]
    lane = lax.broadcasted_iota(jnp.int32, ang.shape, 1)
    first = (lane >= ROPE_LO) & (lane < ROPE_MID)
    second = (lane >= ROPE_MID) & (lane < ROPE_HI)
    c = jnp.cos(ang)
    s = jnp.sin(ang)
    cos_ref[...] = jnp.where(first | second, c, 1.0)
    sina_ref[...] = jnp.where(second, s, 0.0)
    sinb_ref[...] = jnp.where(first, -s, 0.0)


def _rope_tables(positions):
    t = positions.size
    half = MLA_ROPE // 2
    inv = jnp.power(jnp.float32(ROPE_BASE), -jnp.arange(half, dtype=jnp.float32) / half)
    inv_full = jnp.zeros((1, LANES), jnp.float32)
    inv_full = inv_full.at[0, ROPE_LO:ROPE_MID].set(inv).at[0, ROPE_MID:ROPE_HI].set(inv)
    out = jax.ShapeDtypeStruct((t, LANES), jnp.float32)
    return pl.pallas_call(
        _rope_tables_kernel,
        grid=(t // TABLE_TILE,),
        in_specs=[pl.BlockSpec((TABLE_TILE, 1), lambda i: (i, 0)), _const_spec((1, LANES))],
        out_specs=[pl.BlockSpec((TABLE_TILE, LANES), lambda i: (i, 0))] * 3,
        out_shape=[out] * 3,
        compiler_params=_params("arbitrary"),
        name="rope_tables",
    )(positions.reshape(t, 1), inv_full)


def _t5_large_thresholds():
    n = T5_BUCKETS // 2
    max_exact = n // 2
    assert T5_MAX_DIST == 16 * max_exact and n - max_exact == 8
    out = []
    for j in range(1, n - max_exact):
        a = max_exact
        while a * a < max_exact * max_exact * 2 ** j:
            a += 1
        out.append(a)
    return out


def _t5_bias_kernel(table_ref, bias_ref):
    t = bias_ref.shape[-1]
    off = pl.program_id(0) * t
    row = lax.broadcasted_iota(jnp.int32, (t, t), 0)
    col = lax.broadcasted_iota(jnp.int32, (t, t), 1)
    rel = row - col - off
    n = T5_BUCKETS // 2
    max_exact = n // 2
    a = jnp.abs(rel)
    large = jnp.full((t, t), max_exact, jnp.int32)
    for thr in _t5_large_thresholds():
        large = large + (a >= thr).astype(jnp.int32)
    bucket = jnp.where(rel > 0, n, 0) + jnp.where(a < max_exact, a, large)
    far_bucket = n - 1
    for h in range(DIFF_HEADS):
        val = jnp.full((t, t), table_ref[0, h], jnp.float32)
        for b in range(1, T5_BUCKETS):
            val = jnp.where(bucket == b, table_ref[b, h], val)
        bias_ref[0, h] = (val - table_ref[far_bucket, h]) * LOG2E


def _t5_bias_tiles(t5_table):
    t = ATTN_TILE
    assert t >= _t5_large_thresholds()[-1]
    return pl.pallas_call(
        _t5_bias_kernel,
        grid=(2,),
        in_specs=[pl.BlockSpec(memory_space=pltpu.SMEM)],
        out_specs=pl.BlockSpec((1, DIFF_HEADS, t, t), lambda i: (i, 0, 0, 0)),
        out_shape=jax.ShapeDtypeStruct((2, DIFF_HEADS, t, t), jnp.float32),
        compiler_params=_params("arbitrary"),
        name="t5_bias_tiles",
    )(t5_table.astype(jnp.float32))


def _mem_kv_kernel(mem_ref, g_ref, w_ref, gk_ref, km_ref, vm_ref):
    x = mem_ref[0]
    h = _bf16(x * _rms_scale(x, D_MODEL) * g_ref[0])
    kv = _dot(h, w_ref[0])
    width = MEM_HEADS * MEM_HEAD_DIM
    for hd in range(MEM_HEADS):
        sl = slice(hd * MEM_HEAD_DIM, (hd + 1) * MEM_HEAD_DIM)
        k = kv[:, sl]
        km_ref[0, 0, :, sl] = _bf16(k * _rms_scale(k, MEM_HEAD_DIM) * gk_ref[0])
    vm_ref[0, 0] = _bf16(kv[:, width:])


def _mem_kv(mem, g_mem, w_mem_kv, g_mem_k):
    b, m, _ = mem.shape
    width = MEM_HEADS * MEM_HEAD_DIM
    out = jax.ShapeDtypeStruct((DEPTH, b, m, width), jnp.bfloat16)
    return pl.pallas_call(
        _mem_kv_kernel,
        grid=(DEPTH, b),
        in_specs=[
            pl.BlockSpec((1, m, D_MODEL), lambda l, i: (i, 0, 0)),
            pl.BlockSpec((1, 1, D_MODEL), lambda l, i: (l, 0, 0)),
            pl.BlockSpec((1, D_MODEL, 2 * width), lambda l, i: (l, 0, 0)),
            pl.BlockSpec((1, 1, MEM_HEAD_DIM), lambda l, i: (l, 0, 0)),
        ],
        out_specs=[pl.BlockSpec((1, 1, m, width), lambda l, i: (l, i, 0, 0))] * 2,
        out_shape=[out] * 2,
        compiler_params=_params("arbitrary", "arbitrary"),
        name="mem_kv",
    )(mem, g_mem.reshape(DEPTH, 1, D_MODEL), _bf16(w_mem_kv), g_mem_k.reshape(DEPTH, 1, MEM_HEAD_DIM))


def _rope(x, cos_t, sina_t, sinb_t):
    half = MLA_ROPE // 2
    return x * cos_t + pltpu.roll(x, half, 1) * sina_t + pltpu.roll(x, LANES - half, 1) * sinb_t


def _proj_kernel(x_ref, gmix_ref, win_ref, gcq_ref, wuq_ref, gckv_ref, wuk_ref, wuvt_ref, vones_ref,
                 wdvt_ref, dvones_ref, gq_ref, gk_ref, cos_ref, sina_ref, sinb_ref, gdq_ref, gdk_ref,
                 gmq_ref, km_ref, vm_ref,
                 q_ref, k_ref, vt_ref, dq0_ref, dq1_ref, dk_ref, dvt_ref, oc_ref):
    x = x_ref[...]
    h = _bf16(x * _rms_scale(x, D_MODEL) * gmix_ref[...])
    z = _dot(h, win_ref[...])
    cos_t, sina_t, sinb_t = cos_ref[...], sina_ref[...], sinb_ref[...]

    c_q = z[:, _C_CQ:_C_CKV]
    q_raw = _dot(_bf16(c_q * _rms_scale(c_q, Q_LORA) * gcq_ref[...]), wuq_ref[...])
    c_kv = z[:, _C_CKV:_C_KR]
    ckv_n = _bf16(c_kv * _rms_scale(c_kv, KV_LORA) * gckv_ref[...])
    k_nope = _dot(ckv_n, wuk_ref[...])
    vt_ref[...] = _bf16(_dot_nt(wuvt_ref[...], ckv_n) + vones_ref[...])
    k_rope = z[:, _C_KR:_C_DQ]
    gq, gk = gq_ref[...], gk_ref[...]
    for hd in range(MLA_HEADS):
        sl = slice(hd * LANES, (hd + 1) * LANES)
        q = q_raw[:, sl]
        q_ref[:, sl] = _bf16(_rope(q * _rms_scale(q, MLA_QK) * gq, cos_t, sina_t, sinb_t))
        k = k_nope[:, sl] + k_rope
        k_ref[:, sl] = _bf16(_rope(k * _rms_scale(k, MLA_QK) * gk, cos_t, sina_t, sinb_t))

    lane = lax.broadcasted_iota(jnp.int32, (x.shape[0], LANES), 1)
    lo = lane < DIFF_QK
    gdq, gdk = gdq_ref[...], gdk_ref[...]

    def half_norm(t, g):
        t2 = t * t
        s_lo = jnp.sum(jnp.where(lo, t2, 0.0), axis=-1, keepdims=True)
        s_hi = jnp.sum(jnp.where(lo, 0.0, t2), axis=-1, keepdims=True)
        ms = jnp.where(lo, s_lo, s_hi) * (1.0 / DIFF_QK)
        return t * lax.rsqrt(ms + EPS) * g

    for hd in range(DIFF_HEADS):
        sl = slice(hd * LANES, (hd + 1) * LANES)
        dq = half_norm(z[:, _C_DQ + hd * LANES:_C_DQ + (hd + 1) * LANES], gdq)
        dq0_ref[:, sl] = _bf16(jnp.where(lo, dq, 0.0))
        dq1_ref[:, sl] = _bf16(jnp.where(lo, 0.0, dq))
        dk_ref[:, sl] = _bf16(half_norm(z[:, _C_DK + hd * LANES:_C_DK + (hd + 1) * LANES], gdk))
    dvt_ref[...] = _bf16(_dot_nt(wdvt_ref[...], h) + dvones_ref[...])

    gmq = gmq_ref[...]
    for hd in range(MEM_HEADS):
        sl = slice(hd * MEM_HEAD_DIM, (hd + 1) * MEM_HEAD_DIM)
        mq = z[:, _C_MQ + hd * MEM_HEAD_DIM:_C_MQ + (hd + 1) * MEM_HEAD_DIM]
        mq = _bf16(mq * _rms_scale(mq, MEM_HEAD_DIM) * gmq)
        s = _dot_nt(mq, km_ref[0, 0, :, sl])
        p = jnp.exp(s - jnp.max(s, axis=-1, keepdims=True))
        o = _dot(_bf16(p), vm_ref[0, 0, :, sl])
        oc_ref[:, sl] = _bf16(o / jnp.sum(p, axis=-1, keepdims=True))


def _proj(layer, x2, w, tabs, km, vm, seq):
    t = x2.shape[0]
    tm = TOKEN_TILE
    steps_per_batch = seq // tm
    m = km.shape[2]
    width = MEM_HEADS * MEM_HEAD_DIM
    row = lambda i: (i, 0)
    tok = lambda c: pl.BlockSpec((tm, c), row)
    mem_spec = pl.BlockSpec((1, 1, m, width), lambda i: (layer, i // steps_per_batch, 0, 0))
    wide = jax.ShapeDtypeStruct((t, MLA_HEADS * LANES), jnp.bfloat16)
    narrow = jax.ShapeDtypeStruct((t, BRANCH_W), jnp.bfloat16)
    col = lambda i: (0, i)
    vt_rows, dvt_rows = MLA_HEADS * MLA_VROWS, DIFF_HEADS * DIFF_VROWS
    consts = [w["g_mix"], w["w_in"], w["g_cq"], w["w_uq"], w["g_ckv"], w["w_uk"], w["w_uvt"],
              _ones_rows(MLA_HEADS, MLA_VROWS, MLA_V), w["w_dvt"],
              _ones_rows(DIFF_HEADS, DIFF_VROWS, DIFF_V), w["g_q"], w["g_k"]]
    gains = [w["g_dq"], w["g_dk"], w["g_mq"]]
    return pl.pallas_call(
        _proj_kernel,
        grid=(t // tm,),
        in_specs=([tok(D_MODEL)] + [_const_spec(c.shape) for c in consts] + [tok(LANES)] * 3
                  + [_const_spec(c.shape) for c in gains] + [mem_spec, mem_spec]),
        out_specs=([tok(MLA_HEADS * LANES)] * 2 + [pl.BlockSpec((vt_rows, tm), col)] + [tok(BRANCH_W)] * 3
                   + [pl.BlockSpec((dvt_rows, tm), col), tok(BRANCH_W)]),
        out_shape=([wide, wide, jax.ShapeDtypeStruct((vt_rows, t), jnp.bfloat16)] + [narrow] * 3
                   + [jax.ShapeDtypeStruct((dvt_rows, t), jnp.bfloat16), narrow]),
        compiler_params=_params("arbitrary"),
        name="proj",
    )(x2, *consts, *tabs, *gains, km, vm)


def _chunk_mask_t(shape):
    krow = lax.broadcasted_iota(jnp.int32, shape, 0)
    qcol = lax.broadcasted_iota(jnp.int32, shape, 1)
    return (krow // CHUNK) <= (qcol // CHUNK)


def _flash_step(n, score_fn, vt_fn, m_ref, acc_ref):
    def scores_and_max(i):
        s = score_fn(i)
        m_old = m_ref[i]
        m_new = jnp.maximum(m_old, jnp.max(s, axis=0, keepdims=True))
        m_ref[i] = m_new
        return s, m_new, jnp.exp2(m_old - m_new)

    def exponentials(a):
        s, m_new, alpha = a
        return _bf16(jnp.exp2(s - m_new)), alpha

    def value_product(i, b):
        p, alpha = b
        acc_ref[i] = alpha * acc_ref[i] + _dot(vt_fn(i), p)

    stage_a, stage_b = {}, {}
    for it in range(n + 2):
        if it < n:
            stage_a[it] = scores_and_max(it)
        if 0 <= it - 1 < n:
            stage_b[it - 1] = exponentials(stage_a.pop(it - 1))
        if 0 <= it - 2 < n:
            value_product(it - 2, stage_b.pop(it - 2))


def _init_flash(m_ref, acc_ref):
    m_ref[...] = jnp.full(m_ref.shape, NEG, jnp.float32)
    acc_ref[...] = jnp.zeros(acc_ref.shape, jnp.float32)


def _key_tile(ki, ta):
    return pl.ds(pl.multiple_of(ki * ta, ta), ta)


def _mla_attn_kernel(q_ref, k_ref, vt_ref, o_ref, m_ref, acc_ref):
    qi = pl.program_id(1)
    ta = q_ref.shape[0]
    _init_flash(m_ref, acc_ref)

    def tile(ki, masked):
        keys = _key_tile(ki, ta)
        mask = _chunk_mask_t((ta, ta)) if masked else None

        def score(hd):
            sl = slice(hd * LANES, (hd + 1) * LANES)
            s = _dot_nt(k_ref[keys, sl], q_ref[:, sl])
            return jnp.where(mask, s, NEG) if masked else s

        _flash_step(MLA_HEADS, score,
                    lambda hd: vt_ref[hd * MLA_VROWS:(hd + 1) * MLA_VROWS, keys], m_ref, acc_ref)

    def body(ki, carry):
        tile(ki, False)
        return carry

    lax.fori_loop(0, qi, body, 0)
    tile(qi, True)
    outs = []
    for hd in range(MLA_HEADS):
        a = acc_ref[hd]
        outs.append(a[:MLA_V] / a[MLA_V:MLA_V + 1])
    o_ref[...] = _bf16(jnp.concatenate(outs, axis=0).T)


def _mla_attn(q, k, vt, batch, seq):
    ta = ATTN_TILE
    nb = seq // ta
    q_map = lambda b, qi: (b * nb + qi, 0)
    return pl.pallas_call(
        _mla_attn_kernel,
        grid=(batch, nb),
        in_specs=[pl.BlockSpec((ta, MLA_HEADS * LANES), q_map),
                  pl.BlockSpec((seq, MLA_HEADS * LANES), lambda b, qi: (b, 0)),
                  pl.BlockSpec((MLA_HEADS * MLA_VROWS, seq), lambda b, qi: (0, b))],
        out_specs=pl.BlockSpec((ta, BRANCH_W), q_map),
        out_shape=jax.ShapeDtypeStruct((batch * seq, BRANCH_W), jnp.bfloat16),
        scratch_shapes=[pltpu.VMEM((MLA_HEADS, 1, ta), jnp.float32),
                        pltpu.VMEM((MLA_HEADS, MLA_VROWS, ta), jnp.float32)],
        compiler_params=_params("arbitrary", "arbitrary"),
        name="mla_attn",
    )(q, k, vt)


def _diff_attn_kernel(lam_ref, q0_ref, q1_ref, k_ref, vt_ref, bias_ref, gout_ref, o_ref,
                      m_ref, acc_ref, *, out_scale):
    qi = pl.program_id(1)
    ta = q0_ref.shape[0]
    _init_flash(m_ref, acc_ref)

    def tile(ki, bias_tile, masked):
        keys = _key_tile(ki, ta)
        mask = _chunk_mask_t((ta, ta)) if masked else None

        def score(idx):
            hd, q_ref = idx // 2, (q0_ref, q1_ref)[idx % 2]
            sl = slice(hd * LANES, (hd + 1) * LANES)
            s = _dot_nt(k_ref[keys, sl], q_ref[:, sl])
            if bias_tile is not None:
                s = s + bias_ref[bias_tile, hd]
            return jnp.where(mask, s, NEG) if masked else s

        _flash_step(2 * DIFF_HEADS, score,
                    lambda idx: vt_ref[(idx // 2) * DIFF_VROWS:(idx // 2 + 1) * DIFF_VROWS, keys],
                    m_ref, acc_ref)

    def body(ki, carry):
        tile(ki, None, False)
        return carry

    lax.fori_loop(0, qi - 1, body, 0)

    @pl.when(qi > 0)
    def _():
        tile(qi - 1, 1, False)

    tile(qi, 0, True)
    lam = lam_ref[0]
    g = gout_ref[...] * out_scale
    outs = []
    for hd in range(DIFF_HEADS):
        a0, a1 = acc_ref[2 * hd], acc_ref[2 * hd + 1]
        o = a0[:DIFF_V] / a0[DIFF_V:DIFF_V + 1] - lam * (a1[:DIFF_V] / a1[DIFF_V:DIFF_V + 1])
        ms = jnp.sum(o * o, axis=0, keepdims=True) * (1.0 / DIFF_V)
        outs.append(o * lax.rsqrt(ms + EPS) * g)
    o_ref[...] = _bf16(jnp.concatenate(outs, axis=0).T)


def _diff_attn(lam, q0, q1, k, vt, bias, g_out, out_scale, batch, seq):
    ta = ATTN_TILE
    nb = seq // ta
    q_map = lambda b, qi: (b * nb + qi, 0)
    blk = pl.BlockSpec((ta, BRANCH_W), q_map)
    single = lambda a: pl.BlockSpec(a.shape, lambda *_: (0,) * a.ndim, pipeline_mode=pl.Buffered(1))
    return pl.pallas_call(
        functools.partial(_diff_attn_kernel, out_scale=out_scale),
        grid=(batch, nb),
        in_specs=[pl.BlockSpec(memory_space=pltpu.SMEM), blk, blk,
                  pl.BlockSpec((seq, BRANCH_W), lambda b, qi: (b, 0)),
                  pl.BlockSpec((DIFF_HEADS * DIFF_VROWS, seq), lambda b, qi: (0, b)),
                  single(bias), single(g_out)],
        out_specs=blk,
        out_shape=jax.ShapeDtypeStruct((batch * seq, BRANCH_W), jnp.bfloat16),
        scratch_shapes=[pltpu.VMEM((2 * DIFF_HEADS, 1, ta), jnp.float32),
                        pltpu.VMEM((2 * DIFF_HEADS, DIFF_VROWS, ta), jnp.float32)],
        compiler_params=_params("arbitrary", "arbitrary"),
        name="diff_attn",
    )(lam, q0, q1, k, vt, bias, g_out)


def _merge_mlp_kernel(x_ref, oa_ref, ob_ref, oc_ref, gmix_ref, wg_ref, wb_ref, wout_ref,
                      gmlp_ref, w1_ref, w2_ref, out_ref):
    x = x_ref[...]
    h = _bf16(x * _rms_scale(x, D_MODEL) * gmix_ref[...])
    y = None
    for n, o_ref in enumerate((oa_ref, ob_ref, oc_ref)):
        gate = 1.0 / (1.0 + jnp.exp(-_dot(h, wg_ref[:, n * D_MODEL:(n + 1) * D_MODEL])))
        term = gate * _dot(o_ref[...], wb_ref[n])
        y = term if y is None else y + term
    x = x + _dot(_bf16(y), wout_ref[...])
    h2 = _bf16(x * _rms_scale(x, D_MODEL) * gmlp_ref[...])
    for c in range(D_FF // FF_CHUNK):
        f = jnp.maximum(_dot(h2, w1_ref[:, c * FF_CHUNK:(c + 1) * FF_CHUNK]), 0.0)
        x = x + _dot(_bf16(f * f), w2_ref[c * FF_CHUNK:(c + 1) * FF_CHUNK, :])
    out_ref[...] = x


def _merge_mlp(x2, o_a, o_b, o_c, w):
    t = x2.shape[0]
    tm = TOKEN_TILE
    row = lambda i: (i, 0)
    tok = lambda c: pl.BlockSpec((tm, c), row)
    consts = [w["g_mix"], w["w_gate"], w["w_branch"], w["w_out"], w["g_mlp"], w["w_ff1"], w["w_ff2"]]
    single = lambda c: pl.BlockSpec(c.shape, lambda *_: (0,) * c.ndim, pipeline_mode=pl.Buffered(1))
    return pl.pallas_call(
        _merge_mlp_kernel,
        grid=(t // tm,),
        in_specs=[tok(D_MODEL)] + [tok(BRANCH_W)] * 3 + [single(c) for c in consts],
        out_specs=tok(D_MODEL),
        out_shape=jax.ShapeDtypeStruct((t, D_MODEL), jnp.float32),
        compiler_params=_params("arbitrary"),
        name="merge_mlp",
    )(x2, o_a, o_b, o_c, *consts)


def _pad_heads(w, heads, width):
    w = w.reshape(w.shape[:-1] + (heads, width))
    pad = [(0, 0)] * (w.ndim - 1) + [(0, LANES - width)]
    return jnp.pad(w, pad).reshape(w.shape[:-2] + (heads * LANES,))


def _pad_gain(g, scale):
    return jnp.pad(g * scale, ((0, 0), (0, LANES - g.shape[-1])))[:, None, :]


def _transposed_value_weight(w, rows):
    depth, n_in, heads, width = w.shape
    wt = jnp.pad(jnp.transpose(w, (0, 2, 3, 1)), ((0, 0), (0, 0), (0, rows - width), (0, 0)))
    return wt.reshape(depth, heads * rows, n_in)


def _ones_rows(heads, rows, at):
    col = np.zeros((heads, rows, 1), np.float32)
    col[:, at] = 1.0
    return jnp.asarray(col.reshape(heads * rows, 1))


def _layer_weights(w_in, g_mix, g_cq, w_uq, g_ckv, w_ukv, g_mla_q, g_mla_k, g_diff_q, g_diff_k,
                   g_mem_q, w_branch, w_out, g_mlp, w_ff1, w_ff2):
    splits = np.cumsum([Q_LORA, KV_LORA, MLA_ROPE, DIFF_HEADS * 2 * DIFF_QK, DIFF_HEADS * 2 * DIFF_QK,
                        DIFF_HEADS * DIFF_V, MEM_HEADS * MEM_HEAD_DIM]).tolist()
    c_kr, c_dq, c_dv, c_mq, c_gate = splits[1], splits[2], splits[4], splits[5], splits[6]
    w_kr = jnp.pad(w_in[:, :, c_kr:c_dq], ((0, 0), (0, 0), (ROPE_LO, LANES - ROPE_HI)))
    w_cat = jnp.concatenate([w_in[:, :, :c_kr], w_kr, w_in[:, :, c_dq:c_dv], w_in[:, :, c_mq:c_gate]],
                            axis=-1)
    assert w_cat.shape[-1] == _C_END
    w_ukv = w_ukv.reshape(DEPTH, KV_LORA, MLA_HEADS, MLA_NOPE + MLA_V)
    w_uk = _pad_heads(w_ukv[..., :MLA_NOPE].reshape(DEPTH, KV_LORA, -1), MLA_HEADS, MLA_NOPE)
    w_uvt = _transposed_value_weight(w_ukv[..., MLA_NOPE:], MLA_VROWS)
    w_dvt = _transposed_value_weight(
        w_in[:, :, c_dv:c_mq].reshape(DEPTH, D_MODEL, DIFF_HEADS, DIFF_V), DIFF_VROWS)
    vec = lambda g: g[:, None, :]
    return {
        "g_mix": vec(g_mix), "w_in": _bf16(w_cat), "g_cq": vec(g_cq),
        "w_uq": _bf16(_pad_heads(w_uq, MLA_HEADS, MLA_QK)), "g_ckv": vec(g_ckv),
        "w_uk": _bf16(w_uk), "w_uvt": _bf16(w_uvt), "w_dvt": _bf16(w_dvt),
        "g_q": _pad_gain(g_mla_q, MLA_QK ** -0.5 * LOG2E), "g_k": _pad_gain(g_mla_k, 1.0),
        "g_dq": vec(jnp.tile(g_diff_q, (1, 2)) * (DIFF_QK ** -0.5 * LOG2E)),
        "g_dk": vec(jnp.tile(g_diff_k, (1, 2))),
        "g_mq": vec(g_mem_q * MEM_HEAD_DIM ** -0.5),
        "w_gate": _bf16(w_in[:, :, c_gate:]), "w_branch": _bf16(w_branch), "w_out": _bf16(w_out),
        "g_mlp": vec(g_mlp), "w_ff1": _bf16(w_ff1), "w_ff2": _bf16(w_ff2),
    }


def kernel(x, mem, positions, t5_table, g_mix, g_mem, w_in, g_cq, w_uq, g_ckv, w_ukv, g_mla_q, g_mla_k, g_diff_q, g_diff_k, lam_q1, lam_k1, lam_q2, lam_k2, g_diff_out, w_mem_kv, g_mem_q, g_mem_k, w_branch, w_out, g_mlp, w_ff1, w_ff2):
    b, s, _ = x.shape
    assert s % ATTN_TILE == 0 and s % TOKEN_TILE == 0 and (b * s) % TABLE_TILE == 0
    tabs = _rope_tables(positions)
    bias = _t5_bias_tiles(t5_table)
    km, vm = _mem_kv(mem, g_mem, w_mem_kv, g_mem_k)
    weights = _layer_weights(w_in, g_mix, g_cq, w_uq, g_ckv, w_ukv, g_mla_q, g_mla_k, g_diff_q,
                             g_diff_k, g_mem_q, w_branch, w_out, g_mlp, w_ff1, w_ff2)
    f32 = jnp.float32
    lam_init = [0.8 - 0.6 * math.exp(-0.3 * l) for l in range(DEPTH)]
    lam = (jnp.exp(jnp.sum(lam_q1.astype(f32) * lam_k1.astype(f32), axis=-1))
           - jnp.exp(jnp.sum(lam_q2.astype(f32) * lam_k2.astype(f32), axis=-1))
           + jnp.asarray(lam_init, f32))
    x2 = x.reshape(b * s, D_MODEL)
    for l in range(DEPTH):
        w = {name: v[l] for name, v in weights.items()}
        q, k, vt, dq0, dq1, dk, dvt, o_c = _proj(l, x2, w, tabs, km, vm, s)
        o_a = _mla_attn(q, k, vt, b, s)
        o_b = _diff_attn(lam[l:l + 1], dq0, dq1, dk, dvt, bias, g_diff_out[l][:, None],
                         1.0 - lam_init[l], b, s)
        x2 = _merge_mlp(x2, o_a, o_b, o_c, w)
    return x2.reshape(b, s, D_MODEL)
```

```python
import functools
import math

import jax
import jax.numpy as jnp
import numpy as np
from jax import lax
from jax.experimental import pallas as pl
from jax.experimental.pallas import tpu as pltpu

D_MODEL = 1024
DEPTH = 4
CHUNK = 64
MLA_HEADS = 8
MLA_NOPE = 64
MLA_ROPE = 32
MLA_V = 64
MLA_QK = MLA_NOPE + MLA_ROPE
Q_LORA = 384
KV_LORA = 256
ROPE_BASE = 10000.0
DIFF_HEADS = 4
DIFF_QK = 64
DIFF_V = 2 * DIFF_QK
MEM_HEADS = 4
MEM_HEAD_DIM = 128
BRANCH_W = 512
N_BRANCH = 3
D_FF = 4 * D_MODEL
T5_BUCKETS = 32
T5_MAX_DIST = 128
EPS = 1e-6
NEG = -1e30

LANES = 128
BF16_SUBLANES = 16
V7X_VMEM_BYTES = 64 * 1024 * 1024
VMEM_LIMIT = V7X_VMEM_BYTES * 7 // 8

TOKEN_TILE = 512
ATTN_TILE = 512
TABLE_TILE = 4096
FF_CHUNK = 1024
ROPE_HALF = MLA_ROPE // 2


def _round_up(n, m):
    return -(-n // m) * m


MLA_VROWS = _round_up(MLA_V + 1, BF16_SUBLANES)
DIFF_VROWS = _round_up(DIFF_V + 1, BF16_SUBLANES)
MEM_VROWS = _round_up(MEM_HEAD_DIM + 1, BF16_SUBLANES)
LOG2E = math.log2(math.e)

_R_CQ = 0
_R_CKV = _R_CQ + Q_LORA
_R_KR = _R_CKV + KV_LORA
_R_DQ = _R_KR + MLA_ROPE
_R_DK = _R_DQ + DIFF_HEADS * LANES
_R_DV = _R_DK + DIFF_HEADS * LANES
_R_MQ = _R_DV + DIFF_HEADS * DIFF_VROWS
_R_END = _R_MQ + MEM_HEADS * MEM_HEAD_DIM


def _bf16(x):
    return x.astype(jnp.bfloat16)


def _dot(a, b):
    return jnp.dot(a, b, preferred_element_type=jnp.float32)


def _dot_nt(a, b):
    return lax.dot_general(a, b, (((1,), (1,)), ((), ())), preferred_element_type=jnp.float32)


def _rms_scale(x, n):
    return lax.rsqrt(jnp.sum(x * x, axis=-1, keepdims=True) * (1.0 / n) + EPS)


def _sumsq(x):
    return jnp.sum(x * x, axis=0, keepdims=True)


def _col_scale(sumsq, n):
    return lax.rsqrt(sumsq * (1.0 / n) + EPS)


def _const_spec(shape):
    return pl.BlockSpec(shape, lambda *_: (0,) * len(shape))


def _single_spec(a):
    return pl.BlockSpec(a.shape, lambda *_: (0,) * a.ndim, pipeline_mode=pl.Buffered(1))


def _params(*sem):
    return pltpu.CompilerParams(dimension_semantics=sem, vmem_limit_bytes=VMEM_LIMIT)


def _rope_tables_kernel(pos_ref, inv_ref, cos_ref, sin_ref):
    ang = inv_ref[...] * pos_ref[...].astype(jnp.float32)
    cos_ref[...] = jnp.cos(ang)
    sin_ref[...] = jnp.sin(ang)


def _rope_tables(positions):
    t = positions.size
    inv = jnp.power(jnp.float32(ROPE_BASE), -jnp.arange(ROPE_HALF, dtype=jnp.float32) / ROPE_HALF)
    out = jax.ShapeDtypeStruct((ROPE_HALF, t), jnp.float32)
    return pl.pallas_call(
        _rope_tables_kernel,
        grid=(t // TABLE_TILE,),
        in_specs=[pl.BlockSpec((1, TABLE_TILE), lambda i: (0, i)), _const_spec((ROPE_HALF, 1))],
        out_specs=[pl.BlockSpec((ROPE_HALF, TABLE_TILE), lambda i: (0, i))] * 2,
        out_shape=[out] * 2,
        compiler_params=_params("arbitrary"),
        name="rope_tables",
    )(positions.reshape(1, t), inv.reshape(ROPE_HALF, 1))


def _t5_large_thresholds():
    n = T5_BUCKETS // 2
    max_exact = n // 2
    assert T5_MAX_DIST == 16 * max_exact and n - max_exact == 8
    out = []
    for j in range(1, n - max_exact):
        a = max_exact
        while a * a < max_exact * max_exact * 2 ** j:
            a += 1
        out.append(a)
    return out


def _t5_bias_kernel(table_ref, bias_ref):
    t = bias_ref.shape[-1]
    off = pl.program_id(0) * t
    row = lax.broadcasted_iota(jnp.int32, (t, t), 0)
    col = lax.broadcasted_iota(jnp.int32, (t, t), 1)
    rel = row - col - off
    n = T5_BUCKETS // 2
    max_exact = n // 2
    a = jnp.abs(rel)
    large = jnp.full((t, t), max_exact, jnp.int32)
    for thr in _t5_large_thresholds():
        large = large + (a >= thr).astype(jnp.int32)
    bucket = jnp.where(rel > 0, n, 0) + jnp.where(a < max_exact, a, large)
    far_bucket = n - 1
    for h in range(DIFF_HEADS):
        val = jnp.full((t, t), table_ref[0, h], jnp.float32)
        for b in range(1, T5_BUCKETS):
            val = jnp.where(bucket == b, table_ref[b, h], val)
        bias_ref[0, h] = (val - table_ref[far_bucket, h]) * LOG2E


def _t5_bias_tiles(t5_table):
    t = ATTN_TILE
    assert t >= _t5_large_thresholds()[-1]
    return pl.pallas_call(
        _t5_bias_kernel,
        grid=(2,),
        in_specs=[pl.BlockSpec(memory_space=pltpu.SMEM)],
        out_specs=pl.BlockSpec((1, DIFF_HEADS, t, t), lambda i: (i, 0, 0, 0)),
        out_shape=jax.ShapeDtypeStruct((2, DIFF_HEADS, t, t), jnp.float32),
        compiler_params=_params("arbitrary"),
        name="t5_bias_tiles",
    )(t5_table.astype(jnp.float32))


def _mem_kv_kernel(mem_ref, g_ref, wk_ref, wvt_ref, ones_ref, gk_ref, km_ref, vmt_ref):
    x = mem_ref[0]
    h = _bf16(x * _rms_scale(x, D_MODEL) * g_ref[0])
    k = _dot(h, wk_ref[0])
    for hd in range(MEM_HEADS):
        sl = slice(hd * MEM_HEAD_DIM, (hd + 1) * MEM_HEAD_DIM)
        kh = k[:, sl]
        km_ref[0, 0, :, sl] = _bf16(kh * _rms_scale(kh, MEM_HEAD_DIM) * gk_ref[0])
    vmt_ref[0, 0] = _bf16(_dot_nt(wvt_ref[0], h) + ones_ref[...])


def _mem_kv(mem, g_mem, w_mem_kv, g_mem_k):
    b, m, _ = mem.shape
    width = MEM_HEADS * MEM_HEAD_DIM
    rows = MEM_HEADS * MEM_VROWS
    w_k = _bf16(w_mem_kv[:, :, :width])
    w_vt = _bf16(_transposed_value_weight(
        w_mem_kv[:, :, width:].reshape(DEPTH, D_MODEL, MEM_HEADS, MEM_HEAD_DIM), MEM_VROWS))
    ones = _ones_rows(MEM_HEADS, MEM_VROWS, MEM_HEAD_DIM)
    layer = lambda l, i: (l, 0, 0)
    return pl.pallas_call(
        _mem_kv_kernel,
        grid=(DEPTH, b),
        in_specs=[
            pl.BlockSpec((1, m, D_MODEL), lambda l, i: (i, 0, 0)),
            pl.BlockSpec((1, 1, D_MODEL), layer),
            pl.BlockSpec((1, D_MODEL, width), layer),
            pl.BlockSpec((1, rows, D_MODEL), layer),
            _const_spec(ones.shape),
            pl.BlockSpec((1, 1, MEM_HEAD_DIM), layer),
        ],
        out_specs=[pl.BlockSpec((1, 1, m, width), lambda l, i: (l, i, 0, 0)),
                   pl.BlockSpec((1, 1, rows, m), lambda l, i: (l, i, 0, 0))],
        out_shape=[jax.ShapeDtypeStruct((DEPTH, b, m, width), jnp.bfloat16),
                   jax.ShapeDtypeStruct((DEPTH, b, rows, m), jnp.bfloat16)],
        compiler_params=_params("arbitrary", "arbitrary"),
        name="mem_kv",
    )(mem, g_mem.reshape(DEPTH, 1, D_MODEL), w_k, w_vt, ones, g_mem_k.reshape(DEPTH, 1, MEM_HEAD_DIM))


def _rope_rows(x1, x2, cos_t, sin_t):
    return x1 * cos_t - x2 * sin_t, x2 * cos_t + x1 * sin_t


def _proj_kernel(x_ref, gmix_ref, wt_ref, gcq_ref, wuqt_ref, gckv_ref, wukt_ref, wuvt_ref, vones_ref,
                 dvones_ref, gq_ref, gk_ref, cos_ref, sin_ref, gdq_ref, gdk_ref, gmq_ref, km_ref, vmt_ref,
                 qt_ref, k_ref, vt_ref, dq0t_ref, dq1t_ref, dk_ref, dvt_ref, oc_ref):
    x = x_ref[...]
    tm = x.shape[0]
    h = _bf16(x * _rms_scale(x, D_MODEL) * gmix_ref[...])
    zt = _dot_nt(wt_ref[...], h)
    cos_t, sin_t = cos_ref[...], sin_ref[...]

    c_q = zt[_R_CQ:_R_CKV]
    cq_n = _bf16(c_q * _col_scale(_sumsq(c_q), Q_LORA) * gcq_ref[...])
    qt_raw = _dot(wuqt_ref[...], cq_n)
    c_kv = zt[_R_CKV:_R_KR]
    ckv_n = _bf16(c_kv * _col_scale(_sumsq(c_kv), KV_LORA) * gckv_ref[...])
    k_nope = _dot(wukt_ref[...], ckv_n)
    vt_ref[...] = _bf16(_dot(wuvt_ref[...], ckv_n) + vones_ref[...])
    gq, gk = gq_ref[...], gk_ref[...]
    k_r = zt[_R_KR:_R_DQ]
    kr_ss = _sumsq(k_r)
    kr_g = k_r * gk[MLA_NOPE:]
    kr1, kr2 = _rope_rows(kr_g[:ROPE_HALF], kr_g[ROPE_HALF:], cos_t, sin_t)
    zero_pad = jnp.zeros((LANES - MLA_QK, tm), jnp.float32)
    for hd in range(MLA_HEADS):
        rows = slice(hd * LANES, (hd + 1) * LANES)
        q = qt_raw[hd * LANES:hd * LANES + MLA_QK]
        qn = q * _col_scale(_sumsq(q), MLA_QK) * gq
        q1, q2 = _rope_rows(qn[MLA_NOPE:MLA_NOPE + ROPE_HALF], qn[MLA_NOPE + ROPE_HALF:], cos_t, sin_t)
        qt_ref[rows, :] = _bf16(jnp.concatenate([qn[:MLA_NOPE], q1, q2, zero_pad], axis=0))
        kn = k_nope[hd * MLA_NOPE:(hd + 1) * MLA_NOPE]
        rs = _col_scale(_sumsq(kn) + kr_ss, MLA_QK)
        kt = jnp.concatenate([kn * rs * gk[:MLA_NOPE], kr1 * rs, kr2 * rs, zero_pad], axis=0)
        k_ref[:, rows] = _bf16(kt.T)

    gdq, gdk = gdq_ref[...], gdk_ref[...]
    zero_half = jnp.zeros((DIFF_QK, tm), jnp.float32)

    def half_norm(t, g):
        return t * _col_scale(_sumsq(t), DIFF_QK) * g

    for hd in range(DIFF_HEADS):
        rows = slice(hd * LANES, (hd + 1) * LANES)
        r0 = _R_DQ + hd * LANES
        c0 = half_norm(zt[r0:r0 + DIFF_QK], gdq)
        c1 = half_norm(zt[r0 + DIFF_QK:r0 + LANES], gdq)
        dq0t_ref[rows, :] = _bf16(jnp.concatenate([c0, zero_half], axis=0))
        dq1t_ref[rows, :] = _bf16(jnp.concatenate([zero_half, c1], axis=0))
        r0 = _R_DK + hd * LANES
        kt = jnp.concatenate([half_norm(zt[r0:r0 + DIFF_QK], gdk),
                              half_norm(zt[r0 + DIFF_QK:r0 + LANES], gdk)], axis=0)
        dk_ref[:, rows] = _bf16(kt.T)
    dvt_ref[...] = _bf16(zt[_R_DV:_R_MQ] + dvones_ref[...])

    gmq = gmq_ref[...]
    for hd in range(MEM_HEADS):
        sl = slice(hd * MEM_HEAD_DIM, (hd + 1) * MEM_HEAD_DIM)
        mq = zt[_R_MQ + hd * MEM_HEAD_DIM:_R_MQ + (hd + 1) * MEM_HEAD_DIM]
        mq = _bf16(mq * _col_scale(_sumsq(mq), MEM_HEAD_DIM) * gmq)
        s = _dot(km_ref[0, 0, :, sl], mq)
        p = _bf16(jnp.exp(s - jnp.max(s, axis=0, keepdims=True)))
        o = _dot(vmt_ref[0, 0, hd * MEM_VROWS:(hd + 1) * MEM_VROWS, :], p)
        o = o[:MEM_HEAD_DIM] / o[MEM_HEAD_DIM:MEM_HEAD_DIM + 1]
        oc_ref[:, sl] = _bf16(o.T)


def _proj(layer, x2, w, tabs, km, vmt, seq):
    t = x2.shape[0]
    tm = TOKEN_TILE
    steps_per_batch = seq // tm
    tok = lambda c: pl.BlockSpec((tm, c), lambda i: (i, 0))
    tokt = lambda r: pl.BlockSpec((r, tm), lambda i: (0, i))
    mem_map = lambda i: (layer, i // steps_per_batch, 0, 0)
    bf = jnp.bfloat16
    consts = [w["g_mix"], w["w_t"], w["g_cq"], w["w_uqt"], w["g_ckv"], w["w_ukt"], w["w_uvt"],
              _ones_rows(MLA_HEADS, MLA_VROWS, MLA_V), _ones_rows(DIFF_HEADS, DIFF_VROWS, DIFF_V),
              w["g_q"], w["g_k"]]
    gains = [w["g_dq"], w["g_dk"], w["g_mq"]]
    wide, vt_rows, dvt_rows = MLA_HEADS * LANES, MLA_HEADS * MLA_VROWS, DIFF_HEADS * DIFF_VROWS
    outs = [(tokt(wide), (wide, t)), (tok(wide), (t, wide)), (tokt(vt_rows), (vt_rows, t)),
            (tokt(BRANCH_W), (BRANCH_W, t)), (tokt(BRANCH_W), (BRANCH_W, t)), (tok(BRANCH_W), (t, BRANCH_W)),
            (tokt(dvt_rows), (dvt_rows, t)), (tok(BRANCH_W), (t, BRANCH_W))]
    return pl.pallas_call(
        _proj_kernel,
        grid=(t // tm,),
        in_specs=([tok(D_MODEL)] + [_const_spec(c.shape) for c in consts] + [tokt(ROPE_HALF)] * 2
                  + [_const_spec(c.shape) for c in gains]
                  + [pl.BlockSpec((1, 1) + km.shape[2:], mem_map), pl.BlockSpec((1, 1) + vmt.shape[2:], mem_map)]),
        out_specs=[spec for spec, _ in outs],
        out_shape=[jax.ShapeDtypeStruct(shape, bf) for _, shape in outs],
        compiler_params=_params("arbitrary"),
        name="proj",
    )(x2, *consts, *tabs, *gains, km, vmt)


def _chunk_mask_t(shape):
    krow = lax.broadcasted_iota(jnp.int32, shape, 0)
    qcol = lax.broadcasted_iota(jnp.int32, shape, 1)
    return (krow // CHUNK) <= (qcol // CHUNK)


def _flash_step(n, score_fn, vt_fn, m_ref, acc_ref):
    def scores_and_max(i):
        s = score_fn(i)
        m_old = m_ref[i]
        m_new = jnp.maximum(m_old, jnp.max(s, axis=0, keepdims=True))
        m_ref[i] = m_new
        return s, m_new, jnp.exp2(m_old - m_new)

    def exponentials(a):
        s, m_new, alpha = a
        return _bf16(jnp.exp2(s - m_new)), alpha

    def value_product(i, b):
        p, alpha = b
        acc_ref[i] = alpha * acc_ref[i] + _dot(vt_fn(i), p)

    stage_a, stage_b = {}, {}
    for it in range(n + 2):
        if it < n:
            stage_a[it] = scores_and_max(it)
        if 0 <= it - 1 < n:
            stage_b[it - 1] = exponentials(stage_a.pop(it - 1))
        if 0 <= it - 2 < n:
            value_product(it - 2, stage_b.pop(it - 2))


def _init_flash(m_ref, acc_ref):
    m_ref[...] = jnp.full(m_ref.shape, NEG, jnp.float32)
    acc_ref[...] = jnp.zeros(acc_ref.shape, jnp.float32)


def _key_tile(ki, ta):
    return pl.ds(pl.multiple_of(ki * ta, ta), ta)


def _mla_attn_kernel(qt_ref, k_ref, vt_ref, o_ref, m_ref, acc_ref):
    qi = pl.program_id(1)
    ta = qt_ref.shape[1]
    _init_flash(m_ref, acc_ref)

    def tile(ki, masked):
        keys = _key_tile(ki, ta)
        mask = _chunk_mask_t((ta, ta)) if masked else None

        def score(hd):
            sl = slice(hd * LANES, (hd + 1) * LANES)
            s = _dot(k_ref[keys, sl], qt_ref[sl, :])
            return jnp.where(mask, s, NEG) if masked else s

        _flash_step(MLA_HEADS, score,
                    lambda hd: vt_ref[hd * MLA_VROWS:(hd + 1) * MLA_VROWS, keys], m_ref, acc_ref)

    def body(ki, carry):
        tile(ki, False)
        return carry

    lax.fori_loop(0, qi, body, 0)
    tile(qi, True)
    outs = []
    for hd in range(MLA_HEADS):
        a = acc_ref[hd]
        outs.append(a[:MLA_V] / a[MLA_V:MLA_V + 1])
    o_ref[...] = _bf16(jnp.concatenate(outs, axis=0).T)


def _mla_attn(qt, k, vt, batch, seq):
    ta = ATTN_TILE
    nb = seq // ta
    return pl.pallas_call(
        _mla_attn_kernel,
        grid=(batch, nb),
        in_specs=[pl.BlockSpec((MLA_HEADS * LANES, ta), lambda b, qi: (0, b * nb + qi)),
                  pl.BlockSpec((seq, MLA_HEADS * LANES), lambda b, qi: (b, 0)),
                  pl.BlockSpec((MLA_HEADS * MLA_VROWS, seq), lambda b, qi: (0, b))],
        out_specs=pl.BlockSpec((ta, BRANCH_W), lambda b, qi: (b * nb + qi, 0)),
        out_shape=jax.ShapeDtypeStruct((batch * seq, BRANCH_W), jnp.bfloat16),
        scratch_shapes=[pltpu.VMEM((MLA_HEADS, 1, ta), jnp.float32),
                        pltpu.VMEM((MLA_HEADS, MLA_VROWS, ta), jnp.float32)],
        compiler_params=_params("arbitrary", "arbitrary"),
        name="mla_attn",
    )(qt, k, vt)


def _diff_attn_kernel(lam_ref, q0t_ref, q1t_ref, k_ref, vt_ref, bias_ref, gout_ref, o_ref,
                      m_ref, acc_ref, *, out_scale):
    qi = pl.program_id(1)
    ta = q0t_ref.shape[1]
    _init_flash(m_ref, acc_ref)

    def tile(ki, bias_tile, masked):
        keys = _key_tile(ki, ta)
        mask = _chunk_mask_t((ta, ta)) if masked else None

        def score(idx):
            hd, qt_ref = idx // 2, (q0t_ref, q1t_ref)[idx % 2]
            sl = slice(hd * LANES, (hd + 1) * LANES)
            s = _dot(k_ref[keys, sl], qt_ref[sl, :])
            if bias_tile is not None:
                s = s + bias_ref[bias_tile, hd]
            return jnp.where(mask, s, NEG) if masked else s

        _flash_step(2 * DIFF_HEADS, score,
                    lambda idx: vt_ref[(idx // 2) * DIFF_VROWS:(idx // 2 + 1) * DIFF_VROWS, keys],
                    m_ref, acc_ref)

    def body(ki, carry):
        tile(ki, None, False)
        return carry

    lax.fori_loop(0, qi - 1, body, 0)

    @pl.when(qi > 0)
    def _():
        tile(qi - 1, 1, False)

    tile(qi, 0, True)
    lam = lam_ref[0]
    g = gout_ref[...] * out_scale
    outs = []
    for hd in range(DIFF_HEADS):
        a0, a1 = acc_ref[2 * hd], acc_ref[2 * hd + 1]
        o = a0[:DIFF_V] / a0[DIFF_V:DIFF_V + 1] - lam * (a1[:DIFF_V] / a1[DIFF_V:DIFF_V + 1])
        outs.append(o * _col_scale(_sumsq(o), DIFF_V) * g)
    o_ref[...] = _bf16(jnp.concatenate(outs, axis=0).T)


def _diff_attn(lam, q0t, q1t, k, vt, bias, g_out, out_scale, batch, seq):
    ta = ATTN_TILE
    nb = seq // ta
    qt_spec = pl.BlockSpec((BRANCH_W, ta), lambda b, qi: (0, b * nb + qi))
    return pl.pallas_call(
        functools.partial(_diff_attn_kernel, out_scale=out_scale),
        grid=(batch, nb),
        in_specs=[pl.BlockSpec(memory_space=pltpu.SMEM), qt_spec, qt_spec,
                  pl.BlockSpec((seq, BRANCH_W), lambda b, qi: (b, 0)),
                  pl.BlockSpec((DIFF_HEADS * DIFF_VROWS, seq), lambda b, qi: (0, b)),
                  _single_spec(bias), _single_spec(g_out)],
        out_specs=pl.BlockSpec((ta, BRANCH_W), lambda b, qi: (b * nb + qi, 0)),
        out_shape=jax.ShapeDtypeStruct((batch * seq, BRANCH_W), jnp.bfloat16),
        scratch_shapes=[pltpu.VMEM((2 * DIFF_HEADS, 1, ta), jnp.float32),
                        pltpu.VMEM((2 * DIFF_HEADS, DIFF_VROWS, ta), jnp.float32)],
        compiler_params=_params("arbitrary", "arbitrary"),
        name="diff_attn",
    )(lam, q0t, q1t, k, vt, bias, g_out)


def _merge_mlp_kernel(x_ref, oa_ref, ob_ref, oc_ref, gmix_ref, wg_ref, wb_ref, wout_ref,
                      gmlp_ref, w1_ref, w2_ref, out_ref):
    x = x_ref[...]
    h = _bf16(x * _rms_scale(x, D_MODEL) * gmix_ref[...])
    y = None
    for n, o_ref in enumerate((oa_ref, ob_ref, oc_ref)):
        gate = 1.0 / (1.0 + jnp.exp(-_dot(h, wg_ref[:, n * D_MODEL:(n + 1) * D_MODEL])))
        term = gate * _dot(o_ref[...], wb_ref[n])
        y = term if y is None else y + term
    x = x + _dot(_bf16(y), wout_ref[...])
    h2 = _bf16(x * _rms_scale(x, D_MODEL) * gmlp_ref[...])
    for c in range(D_FF // FF_CHUNK):
        f = jnp.maximum(_dot(h2, w1_ref[:, c * FF_CHUNK:(c + 1) * FF_CHUNK]), 0.0)
        x = x + _dot(_bf16(f * f), w2_ref[c * FF_CHUNK:(c + 1) * FF_CHUNK, :])
    out_ref[...] = x


def _merge_mlp(x2, o_a, o_b, o_c, w):
    t = x2.shape[0]
    tm = TOKEN_TILE
    tok = lambda c: pl.BlockSpec((tm, c), lambda i: (i, 0))
    consts = [w["g_mix"], w["w_gate"], w["w_branch"], w["w_out"], w["g_mlp"], w["w_ff1"], w["w_ff2"]]
    return pl.pallas_call(
        _merge_mlp_kernel,
        grid=(t // tm,),
        in_specs=[tok(D_MODEL)] + [tok(BRANCH_W)] * 3 + [_single_spec(c) for c in consts],
        out_specs=tok(D_MODEL),
        out_shape=jax.ShapeDtypeStruct((t, D_MODEL), jnp.float32),
        compiler_params=_params("arbitrary"),
        name="merge_mlp",
    )(x2, o_a, o_b, o_c, *consts)


def _transposed_value_weight(w, rows):
    depth, n_in, heads, width = w.shape
    wt = jnp.pad(jnp.transpose(w, (0, 2, 3, 1)), ((0, 0), (0, 0), (0, rows - width), (0, 0)))
    return wt.reshape(depth, heads * rows, n_in)


def _ones_rows(heads, rows, at):
    col = np.zeros((heads, rows, 1), np.float32)
    col[:, at] = 1.0
    return jnp.asarray(col.reshape(heads * rows, 1))


def _layer_weights(w_in, g_mix, g_cq, w_uq, g_ckv, w_ukv, g_mla_q, g_mla_k, g_diff_q, g_diff_k,
                   g_mem_q, w_branch, w_out, g_mlp, w_ff1, w_ff2):
    splits = np.cumsum([Q_LORA, KV_LORA, MLA_ROPE, DIFF_HEADS * 2 * DIFF_QK, DIFF_HEADS * 2 * DIFF_QK,
                        DIFF_HEADS * DIFF_V, MEM_HEADS * MEM_HEAD_DIM]).tolist()
    c_dv, c_mq, c_gate = splits[4], splits[5], splits[6]
    t_in = jnp.swapaxes(w_in[:, :, :c_gate], 1, 2)
    w_dvt = _transposed_value_weight(
        w_in[:, :, c_dv:c_mq].reshape(DEPTH, D_MODEL, DIFF_HEADS, DIFF_V), DIFF_VROWS)
    w_t = jnp.concatenate([t_in[:, :c_dv], w_dvt, t_in[:, c_mq:]], axis=1)
    assert w_t.shape[1] == _R_END
    w_uqt = _transposed_value_weight(w_uq.reshape(DEPTH, Q_LORA, MLA_HEADS, MLA_QK), LANES)
    w_ukv = w_ukv.reshape(DEPTH, KV_LORA, MLA_HEADS, MLA_NOPE + MLA_V)
    w_ukt = _transposed_value_weight(w_ukv[..., :MLA_NOPE], MLA_NOPE)
    w_uvt = _transposed_value_weight(w_ukv[..., MLA_NOPE:], MLA_VROWS)
    row = lambda g: g[:, None, :]
    col = lambda g: g[:, :, None]
    return {
        "g_mix": row(g_mix), "w_t": _bf16(w_t), "g_cq": col(g_cq), "w_uqt": _bf16(w_uqt),
        "g_ckv": col(g_ckv), "w_ukt": _bf16(w_ukt), "w_uvt": _bf16(w_uvt),
        "g_q": col(g_mla_q * (MLA_QK ** -0.5 * LOG2E)), "g_k": col(g_mla_k),
        "g_dq": col(g_diff_q * (DIFF_QK ** -0.5 * LOG2E)), "g_dk": col(g_diff_k),
        "g_mq": col(g_mem_q * MEM_HEAD_DIM ** -0.5),
        "w_gate": _bf16(w_in[:, :, c_gate:]), "w_branch": _bf16(w_branch), "w_out": _bf16(w_out),
        "g_mlp": row(g_mlp), "w_ff1": _bf16(w_ff1), "w_ff2": _bf16(w_ff2),
    }


def kernel(x, mem, positions, t5_table, g_mix, g_mem, w_in, g_cq, w_uq, g_ckv, w_ukv, g_mla_q, g_mla_k, g_diff_q, g_diff_k, lam_q1, lam_k1, lam_q2, lam_k2, g_diff_out, w_mem_kv, g_mem_q, g_mem_k, w_branch, w_out, g_mlp, w_ff1, w_ff2):
    b, s, _ = x.shape
    assert s % ATTN_TILE == 0 and s % TOKEN_TILE == 0 and (b * s) % TABLE_TILE == 0
    tabs = _rope_tables(positions)
    bias = _t5_bias_tiles(t5_table)
    km, vmt = _mem_kv(mem, g_mem, w_mem_kv, g_mem_k)
    weights = _layer_weights(w_in, g_mix, g_cq, w_uq, g_ckv, w_ukv, g_mla_q, g_mla_k, g_diff_q,
                             g_diff_k, g_mem_q, w_branch, w_out, g_mlp, w_ff1, w_ff2)
    f32 = jnp.float32
    lam_init = [0.8 - 0.6 * math.exp(-0.3 * l) for l in range(DEPTH)]
    lam = (jnp.exp(jnp.sum(lam_q1.astype(f32) * lam_k1.astype(f32), axis=-1))
           - jnp.exp(jnp.sum(lam_q2.astype(f32) * lam_k2.astype(f32), axis=-1))
           + jnp.asarray(lam_init, f32))
    x2 = x.reshape(b * s, D_MODEL)
    for l in range(DEPTH):
        w = {name: v[l] for name, v in weights.items()}
        qt, k, vt, dq0t, dq1t, dk, dvt, o_c = _proj(l, x2, w, tabs, km, vmt, s)
        o_a = _mla_attn(qt, k, vt, b, s)
        o_b = _diff_attn(lam[l:l + 1], dq0t, dq1t, dk, dvt, bias, g_diff_out[l][:, None],
                         1.0 - lam_init[l], b, s)
        x2 = _merge_mlp(x2, o_a, o_b, o_c, w)
    return x2.reshape(b, s, D_MODEL)
```

```python
import functools
import math

import jax
import jax.numpy as jnp
import numpy as np
from jax import lax
from jax.experimental import pallas as pl
from jax.experimental.pallas import tpu as pltpu

D_MODEL = 1024
DEPTH = 4
CHUNK = 64
MLA_HEADS = 8
MLA_NOPE = 64
MLA_ROPE = 32
MLA_V = 64
MLA_QK = MLA_NOPE + MLA_ROPE
Q_LORA = 384
KV_LORA = 256
ROPE_BASE = 10000.0
DIFF_HEADS = 4
DIFF_QK = 64
DIFF_V = 2 * DIFF_QK
MEM_HEADS = 4
MEM_HEAD_DIM = 128
BRANCH_W = 512
N_BRANCH = 3
D_FF = 4 * D_MODEL
T5_BUCKETS = 32
T5_MAX_DIST = 128
EPS = 1e-6
NEG = -1e30

LANES = 128
BF16_SUBLANES = 16
V7X_VMEM_BYTES = 64 * 1024 * 1024
VMEM_LIMIT = V7X_VMEM_BYTES * 7 // 8

TOKEN_TILE = 512
ATTN_TILE = 512
FAR_TILES = 2
TABLE_TILE = 4096
FF_CHUNK = 1024
ROPE_HALF = MLA_ROPE // 2


def _round_up(n, m):
    return -(-n // m) * m


MLA_VROWS = _round_up(MLA_V + 1, BF16_SUBLANES)
DIFF_VROWS = _round_up(DIFF_V + 1, BF16_SUBLANES)
MEM_VROWS = _round_up(MEM_HEAD_DIM + 1, BF16_SUBLANES)
LOG2E = math.log2(math.e)

_R_CQ = 0
_R_CKV = _R_CQ + Q_LORA
_R_KR = _R_CKV + KV_LORA
_R_DQ = _R_KR + MLA_ROPE
_R_DK = _R_DQ + DIFF_HEADS * LANES
_R_DV = _R_DK + DIFF_HEADS * LANES
_R_MQ = _R_DV + DIFF_HEADS * DIFF_VROWS
_R_END = _R_MQ + MEM_HEADS * MEM_HEAD_DIM


def _bf16(x):
    return x.astype(jnp.bfloat16)


def _dot(a, b):
    return jnp.dot(a, b, preferred_element_type=jnp.float32)


def _dot_nt(a, b):
    return lax.dot_general(a, b, (((1,), (1,)), ((), ())), preferred_element_type=jnp.float32)


def _rms_scale(x, n):
    return lax.rsqrt(jnp.sum(x * x, axis=-1, keepdims=True) * (1.0 / n) + EPS)


def _sumsq(x):
    return jnp.sum(x * x, axis=0, keepdims=True)


def _col_scale(sumsq, n):
    return lax.rsqrt(sumsq * (1.0 / n) + EPS)


def _const_spec(shape):
    return pl.BlockSpec(shape, lambda *_: (0,) * len(shape))


def _single_spec(a):
    return pl.BlockSpec(a.shape, lambda *_: (0,) * a.ndim, pipeline_mode=pl.Buffered(1))


def _layer_spec(a, layer, single=False):
    return pl.BlockSpec((None,) + a.shape[1:], lambda *_: (layer,) + (0,) * (a.ndim - 1),
                        pipeline_mode=pl.Buffered(1) if single else None)


def _params(*sem):
    return pltpu.CompilerParams(dimension_semantics=sem, vmem_limit_bytes=VMEM_LIMIT)


def _rope_tables_kernel(pos_ref, inv_ref, cos_ref, sin_ref):
    ang = inv_ref[...] * pos_ref[...].astype(jnp.float32)
    cos_ref[...] = jnp.cos(ang)
    sin_ref[...] = jnp.sin(ang)


def _rope_tables(positions):
    t = positions.size
    inv = jnp.power(jnp.float32(ROPE_BASE), -jnp.arange(ROPE_HALF, dtype=jnp.float32) / ROPE_HALF)
    out = jax.ShapeDtypeStruct((ROPE_HALF, t), jnp.float32)
    return pl.pallas_call(
        _rope_tables_kernel,
        grid=(t // TABLE_TILE,),
        in_specs=[pl.BlockSpec((1, TABLE_TILE), lambda i: (0, i)), _const_spec((ROPE_HALF, 1))],
        out_specs=[pl.BlockSpec((ROPE_HALF, TABLE_TILE), lambda i: (0, i))] * 2,
        out_shape=[out] * 2,
        compiler_params=_params("arbitrary"),
        name="rope_tables",
    )(positions.reshape(1, t), inv.reshape(ROPE_HALF, 1))


def _t5_large_thresholds():
    n = T5_BUCKETS // 2
    max_exact = n // 2
    assert T5_MAX_DIST == 16 * max_exact and n - max_exact == 8
    out = []
    for j in range(1, n - max_exact):
        a = max_exact
        while a * a < max_exact * max_exact * 2 ** j:
            a += 1
        out.append(a)
    return out


def _t5_bias_kernel(table_ref, bias_ref):
    t = bias_ref.shape[-1]
    off = pl.program_id(0) * t
    row = lax.broadcasted_iota(jnp.int32, (t, t), 0)
    col = lax.broadcasted_iota(jnp.int32, (t, t), 1)
    rel = row - col - off
    n = T5_BUCKETS // 2
    max_exact = n // 2
    a = jnp.abs(rel)
    large = jnp.full((t, t), max_exact, jnp.int32)
    for thr in _t5_large_thresholds():
        large = large + (a >= thr).astype(jnp.int32)
    bucket = jnp.where(rel > 0, n, 0) + jnp.where(a < max_exact, a, large)
    far_bucket = n - 1
    for h in range(DIFF_HEADS):
        val = jnp.full((t, t), table_ref[0, h], jnp.float32)
        for b in range(1, T5_BUCKETS):
            val = jnp.where(bucket == b, table_ref[b, h], val)
        bias_ref[0, h] = (val - table_ref[far_bucket, h]) * LOG2E


def _t5_bias_tiles(t5_table):
    t = ATTN_TILE
    assert t >= _t5_large_thresholds()[-1]
    return pl.pallas_call(
        _t5_bias_kernel,
        grid=(2,),
        in_specs=[pl.BlockSpec(memory_space=pltpu.SMEM)],
        out_specs=pl.BlockSpec((1, DIFF_HEADS, t, t), lambda i: (i, 0, 0, 0)),
        out_shape=jax.ShapeDtypeStruct((2, DIFF_HEADS, t, t), jnp.float32),
        compiler_params=_params("arbitrary"),
        name="t5_bias_tiles",
    )(t5_table.astype(jnp.float32))


def _mem_kv_kernel(mem_ref, g_ref, wk_ref, wvt_ref, ones_ref, gk_ref, km_ref, vmt_ref):
    x = mem_ref[0]
    h = _bf16(x * _rms_scale(x, D_MODEL) * g_ref[0])
    k = _dot(h, wk_ref[0])
    for hd in range(MEM_HEADS):
        sl = slice(hd * MEM_HEAD_DIM, (hd + 1) * MEM_HEAD_DIM)
        kh = k[:, sl]
        km_ref[0, 0, :, sl] = _bf16(kh * _rms_scale(kh, MEM_HEAD_DIM) * gk_ref[0])
    vmt_ref[0, 0] = _bf16(_dot_nt(wvt_ref[0], h) + ones_ref[...])


def _mem_kv(mem, g_mem, w_mem_kv, g_mem_k):
    b, m, _ = mem.shape
    width = MEM_HEADS * MEM_HEAD_DIM
    rows = MEM_HEADS * MEM_VROWS
    w_k = _bf16(w_mem_kv[:, :, :width])
    w_vt = _bf16(_transposed_value_weight(
        w_mem_kv[:, :, width:].reshape(DEPTH, D_MODEL, MEM_HEADS, MEM_HEAD_DIM), MEM_VROWS))
    ones = _ones_rows(MEM_HEADS, MEM_VROWS, MEM_HEAD_DIM)
    layer = lambda l, i: (l, 0, 0)
    return pl.pallas_call(
        _mem_kv_kernel,
        grid=(DEPTH, b),
        in_specs=[
            pl.BlockSpec((1, m, D_MODEL), lambda l, i: (i, 0, 0)),
            pl.BlockSpec((1, 1, D_MODEL), layer),
            pl.BlockSpec((1, D_MODEL, width), layer),
            pl.BlockSpec((1, rows, D_MODEL), layer),
            _const_spec(ones.shape),
            pl.BlockSpec((1, 1, MEM_HEAD_DIM), layer),
        ],
        out_specs=[pl.BlockSpec((1, 1, m, width), lambda l, i: (l, i, 0, 0)),
                   pl.BlockSpec((1, 1, rows, m), lambda l, i: (l, i, 0, 0))],
        out_shape=[jax.ShapeDtypeStruct((DEPTH, b, m, width), jnp.bfloat16),
                   jax.ShapeDtypeStruct((DEPTH, b, rows, m), jnp.bfloat16)],
        compiler_params=_params("arbitrary", "arbitrary"),
        name="mem_kv",
    )(mem, g_mem.reshape(DEPTH, 1, D_MODEL), w_k, w_vt, ones, g_mem_k.reshape(DEPTH, 1, MEM_HEAD_DIM))


def _rope_rows(x1, x2, cos_t, sin_t):
    return x1 * cos_t - x2 * sin_t, x2 * cos_t + x1 * sin_t


def _proj_kernel(x_ref, gmix_ref, wt_ref, gcq_ref, wuqt_ref, gckv_ref, wukt_ref, wuvt_ref, vones_ref,
                 dvones_ref, gq_ref, gk_ref, cos_ref, sin_ref, gdq_ref, gdk_ref, gmq_ref, km_ref, vmt_ref,
                 qt_ref, k_ref, vt_ref, dq0t_ref, dq1t_ref, dk_ref, dvt_ref, oc_ref):
    x = x_ref[...]
    tm = x.shape[0]
    h = _bf16(x * _rms_scale(x, D_MODEL) * gmix_ref[...])
    zt = _dot_nt(wt_ref[...], h)
    cos_t, sin_t = cos_ref[...], sin_ref[...]

    c_q = zt[_R_CQ:_R_CKV]
    cq_n = _bf16(c_q * _col_scale(_sumsq(c_q), Q_LORA) * gcq_ref[...])
    qt_raw = _dot(wuqt_ref[...], cq_n)
    c_kv = zt[_R_CKV:_R_KR]
    ckv_n = _bf16(c_kv * _col_scale(_sumsq(c_kv), KV_LORA) * gckv_ref[...])
    k_nope = _dot(wukt_ref[...], ckv_n)
    vt_ref[...] = _bf16(_dot(wuvt_ref[...], ckv_n) + vones_ref[...])
    gq, gk = gq_ref[...], gk_ref[...]
    k_r = zt[_R_KR:_R_DQ]
    kr_ss = _sumsq(k_r)
    kr_g = k_r * gk[MLA_NOPE:]
    kr1, kr2 = _rope_rows(kr_g[:ROPE_HALF], kr_g[ROPE_HALF:], cos_t, sin_t)
    zero_pad = jnp.zeros((LANES - MLA_QK, tm), jnp.float32)
    for hd in range(MLA_HEADS):
        rows = slice(hd * LANES, (hd + 1) * LANES)
        q = qt_raw[hd * LANES:hd * LANES + MLA_QK]
        qn = q * _col_scale(_sumsq(q), MLA_QK) * gq
        q1, q2 = _rope_rows(qn[MLA_NOPE:MLA_NOPE + ROPE_HALF], qn[MLA_NOPE + ROPE_HALF:], cos_t, sin_t)
        qt_ref[rows, :] = _bf16(jnp.concatenate([qn[:MLA_NOPE], q1, q2, zero_pad], axis=0))
        kn = k_nope[hd * MLA_NOPE:(hd + 1) * MLA_NOPE]
        rs = _col_scale(_sumsq(kn) + kr_ss, MLA_QK)
        kt = jnp.concatenate([kn * rs * gk[:MLA_NOPE], kr1 * rs, kr2 * rs, zero_pad], axis=0)
        k_ref[:, rows] = _bf16(kt.T)

    gdq, gdk = gdq_ref[...], gdk_ref[...]
    zero_half = jnp.zeros((DIFF_QK, tm), jnp.float32)

    def half_norm(t, g):
        return t * _col_scale(_sumsq(t), DIFF_QK) * g

    for hd in range(DIFF_HEADS):
        rows = slice(hd * LANES, (hd + 1) * LANES)
        r0 = _R_DQ + hd * LANES
        c0 = half_norm(zt[r0:r0 + DIFF_QK], gdq)
        c1 = half_norm(zt[r0 + DIFF_QK:r0 + LANES], gdq)
        dq0t_ref[rows, :] = _bf16(jnp.concatenate([c0, zero_half], axis=0))
        dq1t_ref[rows, :] = _bf16(jnp.concatenate([zero_half, c1], axis=0))
        r0 = _R_DK + hd * LANES
        kt = jnp.concatenate([half_norm(zt[r0:r0 + DIFF_QK], gdk),
                              half_norm(zt[r0 + DIFF_QK:r0 + LANES], gdk)], axis=0)
        dk_ref[:, rows] = _bf16(kt.T)
    dvt_ref[...] = _bf16(zt[_R_DV:_R_MQ] + dvones_ref[...])

    gmq = gmq_ref[...]
    for hd in range(MEM_HEADS):
        sl = slice(hd * MEM_HEAD_DIM, (hd + 1) * MEM_HEAD_DIM)
        mq = zt[_R_MQ + hd * MEM_HEAD_DIM:_R_MQ + (hd + 1) * MEM_HEAD_DIM]
        mq = _bf16(mq * _col_scale(_sumsq(mq), MEM_HEAD_DIM) * gmq)
        s = _dot(km_ref[0, 0, :, sl], mq)
        p = _bf16(jnp.exp(s - jnp.max(s, axis=0, keepdims=True)))
        o = _dot(vmt_ref[0, 0, hd * MEM_VROWS:(hd + 1) * MEM_VROWS, :], p)
        o = o[:MEM_HEAD_DIM] / o[MEM_HEAD_DIM:MEM_HEAD_DIM + 1]
        oc_ref[:, sl] = _bf16(o.T)


def _proj(layer, x2, w, tabs, km, vmt, seq):
    t = x2.shape[0]
    tm = TOKEN_TILE
    steps_per_batch = seq // tm
    tok = lambda c: pl.BlockSpec((tm, c), lambda i: (i, 0))
    tokt = lambda r: pl.BlockSpec((r, tm), lambda i: (0, i))
    mem_map = lambda i: (layer, i // steps_per_batch, 0, 0)
    bf = jnp.bfloat16
    stacked = lambda name: (w[name], _layer_spec(w[name], layer))
    fixed = lambda a: (a, _const_spec(a.shape))
    consts = [stacked("g_mix"), stacked("w_t"), stacked("g_cq"), stacked("w_uqt"), stacked("g_ckv"),
              stacked("w_ukt"), stacked("w_uvt"), fixed(_ones_rows(MLA_HEADS, MLA_VROWS, MLA_V)),
              fixed(_ones_rows(DIFF_HEADS, DIFF_VROWS, DIFF_V)), stacked("g_q"), stacked("g_k")]
    gains = [stacked("g_dq"), stacked("g_dk"), stacked("g_mq")]
    wide, vt_rows, dvt_rows = MLA_HEADS * LANES, MLA_HEADS * MLA_VROWS, DIFF_HEADS * DIFF_VROWS
    outs = [(tokt(wide), (wide, t)), (tok(wide), (t, wide)), (tokt(vt_rows), (vt_rows, t)),
            (tokt(BRANCH_W), (BRANCH_W, t)), (tokt(BRANCH_W), (BRANCH_W, t)), (tok(BRANCH_W), (t, BRANCH_W)),
            (tokt(dvt_rows), (dvt_rows, t)), (tok(BRANCH_W), (t, BRANCH_W))]
    return pl.pallas_call(
        _proj_kernel,
        grid=(t // tm,),
        in_specs=([tok(D_MODEL)] + [spec for _, spec in consts] + [tokt(ROPE_HALF)] * 2
                  + [spec for _, spec in gains]
                  + [pl.BlockSpec((1, 1) + km.shape[2:], mem_map), pl.BlockSpec((1, 1) + vmt.shape[2:], mem_map)]),
        out_specs=[spec for spec, _ in outs],
        out_shape=[jax.ShapeDtypeStruct(shape, bf) for _, shape in outs],
        compiler_params=_params("arbitrary"),
        name="proj",
    )(x2, *[a for a, _ in consts], *tabs, *[a for a, _ in gains], km, vmt)


def _chunk_mask_t(shape):
    krow = lax.broadcasted_iota(jnp.int32, shape, 0)
    qcol = lax.broadcasted_iota(jnp.int32, shape, 1)
    return (krow // CHUNK) <= (qcol // CHUNK)


def _flash_step(n, score_fn, vt_fn, m_ref, acc_ref):
    def scores_and_max(i):
        s = score_fn(i)
        m_old = m_ref[i]
        m_new = jnp.maximum(m_old, jnp.max(s, axis=0, keepdims=True))
        m_ref[i] = m_new
        return s, m_new, jnp.exp2(m_old - m_new)

    def exponentials(a):
        s, m_new, alpha = a
        return _bf16(jnp.exp2(s - m_new)), alpha

    def value_product(i, b):
        p, alpha = b
        acc_ref[i] = alpha * acc_ref[i] + _dot(vt_fn(i), p)

    stage_a, stage_b = {}, {}
    for it in range(n + 2):
        if it < n:
            stage_a[it] = scores_and_max(it)
        if 0 <= it - 1 < n:
            stage_b[it - 1] = exponentials(stage_a.pop(it - 1))
        if 0 <= it - 2 < n:
            value_product(it - 2, stage_b.pop(it - 2))


def _init_flash(m_ref, acc_ref):
    m_ref[...] = jnp.full(m_ref.shape, NEG, jnp.float32)
    acc_ref[...] = jnp.zeros(acc_ref.shape, jnp.float32)


def _key_tile(start_tile, n_tiles, ta):
    return pl.ds(pl.multiple_of(start_tile * ta, ta), n_tiles * ta)


def _for_unmasked_tiles(n, tile):
    def body(i, carry):
        tile(i * FAR_TILES, FAR_TILES)
        return carry

    n = jnp.maximum(n, 0)
    lax.fori_loop(0, n // FAR_TILES, body, 0)
    for r in range(1, FAR_TILES):
        @pl.when(n % FAR_TILES >= r)
        def _():
            tile(n - n % FAR_TILES + (r - 1), 1)


def _mla_attn_kernel(qt_ref, k_ref, vt_ref, o_ref, m_ref, acc_ref):
    qi = pl.program_id(1)
    ta = qt_ref.shape[1]
    _init_flash(m_ref, acc_ref)

    def tile(start, n_tiles, masked=False):
        keys = _key_tile(start, n_tiles, ta)
        mask = _chunk_mask_t((ta, ta)) if masked else None

        def score(hd):
            sl = slice(hd * LANES, (hd + 1) * LANES)
            s = _dot(k_ref[keys, sl], qt_ref[sl, :])
            return jnp.where(mask, s, NEG) if masked else s

        _flash_step(MLA_HEADS, score,
                    lambda hd: vt_ref[hd * MLA_VROWS:(hd + 1) * MLA_VROWS, keys], m_ref, acc_ref)

    _for_unmasked_tiles(qi, tile)
    tile(qi, 1, masked=True)
    outs = []
    for hd in range(MLA_HEADS):
        a = acc_ref[hd]
        outs.append(a[:MLA_V] / a[MLA_V:MLA_V + 1])
    o_ref[...] = _bf16(jnp.concatenate(outs, axis=0).T)


def _mla_attn(qt, k, vt, batch, seq):
    ta = ATTN_TILE
    nb = seq // ta
    return pl.pallas_call(
        _mla_attn_kernel,
        grid=(batch, nb),
        in_specs=[pl.BlockSpec((MLA_HEADS * LANES, ta), lambda b, qi: (0, b * nb + qi)),
                  pl.BlockSpec((seq, MLA_HEADS * LANES), lambda b, qi: (b, 0)),
                  pl.BlockSpec((MLA_HEADS * MLA_VROWS, seq), lambda b, qi: (0, b))],
        out_specs=pl.BlockSpec((ta, BRANCH_W), lambda b, qi: (b * nb + qi, 0)),
        out_shape=jax.ShapeDtypeStruct((batch * seq, BRANCH_W), jnp.bfloat16),
        scratch_shapes=[pltpu.VMEM((MLA_HEADS, 1, ta), jnp.float32),
                        pltpu.VMEM((MLA_HEADS, MLA_VROWS, ta), jnp.float32)],
        compiler_params=_params("arbitrary", "arbitrary"),
        name="mla_attn",
    )(qt, k, vt)


def _diff_attn_kernel(lam_ref, q0t_ref, q1t_ref, k_ref, vt_ref, bias_ref, gout_ref, o_ref,
                      m_ref, acc_ref, *, layer, out_scale):
    qi = pl.program_id(1)
    ta = q0t_ref.shape[1]
    _init_flash(m_ref, acc_ref)

    def tile(start, n_tiles, bias_tile=None, masked=False):
        keys = _key_tile(start, n_tiles, ta)
        mask = _chunk_mask_t((ta, ta)) if masked else None

        def score(idx):
            hd, qt_ref = idx // 2, (q0t_ref, q1t_ref)[idx % 2]
            sl = slice(hd * LANES, (hd + 1) * LANES)
            s = _dot(k_ref[keys, sl], qt_ref[sl, :])
            if bias_tile is not None:
                s = s + bias_ref[bias_tile, hd]
            return jnp.where(mask, s, NEG) if masked else s

        _flash_step(2 * DIFF_HEADS, score,
                    lambda idx: vt_ref[(idx // 2) * DIFF_VROWS:(idx // 2 + 1) * DIFF_VROWS, keys],
                    m_ref, acc_ref)

    _for_unmasked_tiles(qi - 1, tile)

    @pl.when(qi > 0)
    def _():
        tile(qi - 1, 1, bias_tile=1)

    tile(qi, 1, bias_tile=0, masked=True)
    lam = lam_ref[layer]
    g = gout_ref[...] * out_scale
    outs = []
    for hd in range(DIFF_HEADS):
        a0, a1 = acc_ref[2 * hd], acc_ref[2 * hd + 1]
        o = a0[:DIFF_V] / a0[DIFF_V:DIFF_V + 1] - lam * (a1[:DIFF_V] / a1[DIFF_V:DIFF_V + 1])
        outs.append(o * _col_scale(_sumsq(o), DIFF_V) * g)
    o_ref[...] = _bf16(jnp.concatenate(outs, axis=0).T)


def _diff_attn(layer, lam, q0t, q1t, k, vt, bias, g_out, out_scale, batch, seq):
    ta = ATTN_TILE
    nb = seq // ta
    qt_spec = pl.BlockSpec((BRANCH_W, ta), lambda b, qi: (0, b * nb + qi))
    return pl.pallas_call(
        functools.partial(_diff_attn_kernel, layer=layer, out_scale=out_scale),
        grid=(batch, nb),
        in_specs=[pl.BlockSpec(memory_space=pltpu.SMEM), qt_spec, qt_spec,
                  pl.BlockSpec((seq, BRANCH_W), lambda b, qi: (b, 0)),
                  pl.BlockSpec((DIFF_HEADS * DIFF_VROWS, seq), lambda b, qi: (0, b)),
                  _single_spec(bias), _layer_spec(g_out, layer, single=True)],
        out_specs=pl.BlockSpec((ta, BRANCH_W), lambda b, qi: (b * nb + qi, 0)),
        out_shape=jax.ShapeDtypeStruct((batch * seq, BRANCH_W), jnp.bfloat16),
        scratch_shapes=[pltpu.VMEM((2 * DIFF_HEADS, 1, ta), jnp.float32),
                        pltpu.VMEM((2 * DIFF_HEADS, DIFF_VROWS, ta), jnp.float32)],
        compiler_params=_params("arbitrary", "arbitrary"),
        name="diff_attn",
    )(lam, q0t, q1t, k, vt, bias, g_out)


def _merge_mlp_kernel(x_ref, oa_ref, ob_ref, oc_ref, gmix_ref, wg_ref, wb_ref, wout_ref,
                      gmlp_ref, w1_ref, w2_ref, out_ref):
    x = x_ref[...]
    h = _bf16(x * _rms_scale(x, D_MODEL) * gmix_ref[...])
    y = None
    for n, o_ref in enumerate((oa_ref, ob_ref, oc_ref)):
        gate = 1.0 / (1.0 + jnp.exp(-_dot(h, wg_ref[:, n * D_MODEL:(n + 1) * D_MODEL])))
        term = gate * _dot(o_ref[...], wb_ref[n])
        y = term if y is None else y + term
    x = x + _dot(_bf16(y), wout_ref[...])
    h2 = _bf16(x * _rms_scale(x, D_MODEL) * gmlp_ref[...])
    for c in range(D_FF // FF_CHUNK):
        f = jnp.maximum(_dot(h2, w1_ref[:, c * FF_CHUNK:(c + 1) * FF_CHUNK]), 0.0)
        x = x + _dot(_bf16(f * f), w2_ref[c * FF_CHUNK:(c + 1) * FF_CHUNK, :])
    out_ref[...] = x


def _merge_mlp(layer, x2, o_a, o_b, o_c, w):
    t = x2.shape[0]
    tm = TOKEN_TILE
    tok = lambda c: pl.BlockSpec((tm, c), lambda i: (i, 0))
    consts = [w["g_mix"], w["w_gate"], w["w_branch"], w["w_out"], w["g_mlp"], w["w_ff1"], w["w_ff2"]]
    return pl.pallas_call(
        _merge_mlp_kernel,
        grid=(t // tm,),
        in_specs=([tok(D_MODEL)] + [tok(BRANCH_W)] * 3
                  + [_layer_spec(c, layer, single=True) for c in consts]),
        out_specs=tok(D_MODEL),
        out_shape=jax.ShapeDtypeStruct((t, D_MODEL), jnp.float32),
        compiler_params=_params("arbitrary"),
        name="merge_mlp",
    )(x2, o_a, o_b, o_c, *consts)


def _transposed_value_weight(w, rows):
    depth, n_in, heads, width = w.shape
    wt = jnp.pad(jnp.transpose(w, (0, 2, 3, 1)), ((0, 0), (0, 0), (0, rows - width), (0, 0)))
    return wt.reshape(depth, heads * rows, n_in)


def _ones_rows(heads, rows, at):
    col = np.zeros((heads, rows, 1), np.float32)
    col[:, at] = 1.0
    return jnp.asarray(col.reshape(heads * rows, 1))


def _layer_weights(w_in, g_mix, g_cq, w_uq, g_ckv, w_ukv, g_mla_q, g_mla_k, g_diff_q, g_diff_k,
                   g_mem_q, w_branch, w_out, g_mlp, w_ff1, w_ff2):
    splits = np.cumsum([Q_LORA, KV_LORA, MLA_ROPE, DIFF_HEADS * 2 * DIFF_QK, DIFF_HEADS * 2 * DIFF_QK,
                        DIFF_HEADS * DIFF_V, MEM_HEADS * MEM_HEAD_DIM]).tolist()
    c_dv, c_mq, c_gate = splits[4], splits[5], splits[6]
    t_in = jnp.swapaxes(w_in[:, :, :c_gate], 1, 2)
    w_dvt = _transposed_value_weight(
        w_in[:, :, c_dv:c_mq].reshape(DEPTH, D_MODEL, DIFF_HEADS, DIFF_V), DIFF_VROWS)
    w_t = jnp.concatenate([t_in[:, :c_dv], w_dvt, t_in[:, c_mq:]], axis=1)
    assert w_t.shape[1] == _R_END
    w_uqt = _transposed_value_weight(w_uq.reshape(DEPTH, Q_LORA, MLA_HEADS, MLA_QK), LANES)
    w_ukv = w_ukv.reshape(DEPTH, KV_LORA, MLA_HEADS, MLA_NOPE + MLA_V)
    w_ukt = _transposed_value_weight(w_ukv[..., :MLA_NOPE], MLA_NOPE)
    w_uvt = _transposed_value_weight(w_ukv[..., MLA_NOPE:], MLA_VROWS)
    row = lambda g: g[:, None, :]
    col = lambda g: g[:, :, None]
    return {
        "g_mix": row(g_mix), "w_t": _bf16(w_t), "g_cq": col(g_cq), "w_uqt": _bf16(w_uqt),
        "g_ckv": col(g_ckv), "w_ukt": _bf16(w_ukt), "w_uvt": _bf16(w_uvt),
        "g_q": col(g_mla_q * (MLA_QK ** -0.5 * LOG2E)), "g_k": col(g_mla_k),
        "g_dq": col(g_diff_q * (DIFF_QK ** -0.5 * LOG2E)), "g_dk": col(g_diff_k),
        "g_mq": col(g_mem_q * MEM_HEAD_DIM ** -0.5),
        "w_gate": _bf16(w_in[:, :, c_gate:]), "w_branch": _bf16(w_branch), "w_out": _bf16(w_out),
        "g_mlp": row(g_mlp), "w_ff1": _bf16(w_ff1), "w_ff2": _bf16(w_ff2),
    }


def kernel(x, mem, positions, t5_table, g_mix, g_mem, w_in, g_cq, w_uq, g_ckv, w_ukv, g_mla_q, g_mla_k, g_diff_q, g_diff_k, lam_q1, lam_k1, lam_q2, lam_k2, g_diff_out, w_mem_kv, g_mem_q, g_mem_k, w_branch, w_out, g_mlp, w_ff1, w_ff2):
    b, s, _ = x.shape
    assert s % ATTN_TILE == 0 and s % TOKEN_TILE == 0 and (b * s) % TABLE_TILE == 0
    tabs = _rope_tables(positions)
    bias = _t5_bias_tiles(t5_table)
    km, vmt = _mem_kv(mem, g_mem, w_mem_kv, g_mem_k)
    weights = _layer_weights(w_in, g_mix, g_cq, w_uq, g_ckv, w_ukv, g_mla_q, g_mla_k, g_diff_q,
                             g_diff_k, g_mem_q, w_branch, w_out, g_mlp, w_ff1, w_ff2)
    f32 = jnp.float32
    lam_init = [0.8 - 0.6 * math.exp(-0.3 * l) for l in range(DEPTH)]
    lam = (jnp.exp(jnp.sum(lam_q1.astype(f32) * lam_k1.astype(f32), axis=-1))
           - jnp.exp(jnp.sum(lam_q2.astype(f32) * lam_k2.astype(f32), axis=-1))
           + jnp.asarray(lam_init, f32))
    x2 = x.reshape(b * s, D_MODEL)
    g_out = g_diff_out[:, :, None]
    for l in range(DEPTH):
        qt, k, vt, dq0t, dq1t, dk, dvt, o_c = _proj(l, x2, weights, tabs, km, vmt, s)
        o_a = _mla_attn(qt, k, vt, b, s)
        o_b = _diff_attn(l, lam, dq0t, dq1t, dk, dvt, bias, g_out, 1.0 - lam_init[l], b, s)
        x2 = _merge_mlp(l, x2, o_a, o_b, o_c, weights)
    return x2.reshape(b, s, D_MODEL)
```

```python
import functools
import math

import jax
import jax.numpy as jnp
import numpy as np
from jax import lax
from jax.experimental import pallas as pl
from jax.experimental.pallas import tpu as pltpu

D_MODEL = 1024
DEPTH = 4
CHUNK = 64
MLA_HEADS = 8
MLA_NOPE = 64
MLA_ROPE = 32
MLA_V = 64
MLA_QK = MLA_NOPE + MLA_ROPE
Q_LORA = 384
KV_LORA = 256
ROPE_BASE = 10000.0
DIFF_HEADS = 4
DIFF_QK = 64
DIFF_V = 2 * DIFF_QK
MEM_HEADS = 4
MEM_HEAD_DIM = 128
BRANCH_W = 512
N_BRANCH = 3
D_FF = 4 * D_MODEL
T5_BUCKETS = 32
T5_MAX_DIST = 128
EPS = 1e-6
NEG = -1e30

LANES = 128
BF16_SUBLANES = 16
V7X_VMEM_BYTES = 64 * 1024 * 1024
VMEM_LIMIT = V7X_VMEM_BYTES * 7 // 8

TOKEN_TILE = 512
ATTN_TILE = 512
FAR_TILES = 2
TABLE_TILE = 4096
FF_CHUNK = 1024
ROPE_HALF = MLA_ROPE // 2


def _round_up(n, m):
    return -(-n // m) * m


MLA_VROWS = _round_up(MLA_V + 1, BF16_SUBLANES)
DIFF_VROWS = _round_up(DIFF_V + 1, BF16_SUBLANES)
MEM_VROWS = _round_up(MEM_HEAD_DIM + 1, BF16_SUBLANES)
LOG2E = math.log2(math.e)

SAFE_LOGIT_SPREAD = 120.0
BOUND_MARGIN = 1.01
F32_SUBLANES = 8
_N_MLA_Q = 0
_N_MLA_K = _N_MLA_Q + MLA_HEADS
_N_DIFF_Q = _N_MLA_K + MLA_HEADS
_N_DIFF_K = _N_DIFF_Q + 2 * DIFF_HEADS
NORM_ROWS = _N_DIFF_K + 2 * DIFF_HEADS

_R_CQ = 0
_R_CKV = _R_CQ + Q_LORA
_R_KR = _R_CKV + KV_LORA
_R_DQ = _R_KR + MLA_ROPE
_R_DK = _R_DQ + DIFF_HEADS * LANES
_R_DV = _R_DK + DIFF_HEADS * LANES
_R_MQ = _R_DV + DIFF_HEADS * DIFF_VROWS
_R_END = _R_MQ + MEM_HEADS * MEM_HEAD_DIM


def _bf16(x):
    return x.astype(jnp.bfloat16)


def _dot(a, b):
    return jnp.dot(a, b, preferred_element_type=jnp.float32)


def _dot_nt(a, b):
    return lax.dot_general(a, b, (((1,), (1,)), ((), ())), preferred_element_type=jnp.float32)


def _rms_scale(x, n):
    return lax.rsqrt(jnp.sum(x * x, axis=-1, keepdims=True) * (1.0 / n) + EPS)


def _sumsq(x):
    return jnp.sum(x * x, axis=0, keepdims=True)


def _col_scale(sumsq, n):
    return lax.rsqrt(sumsq * (1.0 / n) + EPS)


def _const_spec(shape):
    return pl.BlockSpec(shape, lambda *_: (0,) * len(shape))


def _single_spec(a):
    return pl.BlockSpec(a.shape, lambda *_: (0,) * a.ndim, pipeline_mode=pl.Buffered(1))


def _layer_spec(a, layer, single=False):
    return pl.BlockSpec((None,) + a.shape[1:], lambda *_: (layer,) + (0,) * (a.ndim - 1),
                        pipeline_mode=pl.Buffered(1) if single else None)


def _params(*sem):
    return pltpu.CompilerParams(dimension_semantics=sem, vmem_limit_bytes=VMEM_LIMIT)


def _rope_tables_kernel(pos_ref, inv_ref, cos_ref, sin_ref):
    ang = inv_ref[...] * pos_ref[...].astype(jnp.float32)
    cos_ref[...] = jnp.cos(ang)
    sin_ref[...] = jnp.sin(ang)


def _rope_tables(positions):
    t = positions.size
    inv = jnp.power(jnp.float32(ROPE_BASE), -jnp.arange(ROPE_HALF, dtype=jnp.float32) / ROPE_HALF)
    out = jax.ShapeDtypeStruct((ROPE_HALF, t), jnp.float32)
    return pl.pallas_call(
        _rope_tables_kernel,
        grid=(t // TABLE_TILE,),
        in_specs=[pl.BlockSpec((1, TABLE_TILE), lambda i: (0, i)), _const_spec((ROPE_HALF, 1))],
        out_specs=[pl.BlockSpec((ROPE_HALF, TABLE_TILE), lambda i: (0, i))] * 2,
        out_shape=[out] * 2,
        compiler_params=_params("arbitrary"),
        name="rope_tables",
    )(positions.reshape(1, t), inv.reshape(ROPE_HALF, 1))


def _t5_large_thresholds():
    n = T5_BUCKETS // 2
    max_exact = n // 2
    assert T5_MAX_DIST == 16 * max_exact and n - max_exact == 8
    out = []
    for j in range(1, n - max_exact):
        a = max_exact
        while a * a < max_exact * max_exact * 2 ** j:
            a += 1
        out.append(a)
    return out


def _t5_bias_kernel(table_ref, bias_ref):
    t = bias_ref.shape[-1]
    off = pl.program_id(0) * t
    row = lax.broadcasted_iota(jnp.int32, (t, t), 0)
    col = lax.broadcasted_iota(jnp.int32, (t, t), 1)
    rel = row - col - off
    n = T5_BUCKETS // 2
    max_exact = n // 2
    a = jnp.abs(rel)
    large = jnp.full((t, t), max_exact, jnp.int32)
    for thr in _t5_large_thresholds():
        large = large + (a >= thr).astype(jnp.int32)
    bucket = jnp.where(rel > 0, n, 0) + jnp.where(a < max_exact, a, large)
    far_bucket = n - 1
    for h in range(DIFF_HEADS):
        val = jnp.full((t, t), table_ref[0, h], jnp.float32)
        for b in range(1, T5_BUCKETS):
            val = jnp.where(bucket == b, table_ref[b, h], val)
        bias_ref[0, h] = (val - table_ref[far_bucket, h]) * LOG2E


def _t5_bias_tiles(t5_table):
    t = ATTN_TILE
    assert t >= _t5_large_thresholds()[-1]
    return pl.pallas_call(
        _t5_bias_kernel,
        grid=(2,),
        in_specs=[pl.BlockSpec(memory_space=pltpu.SMEM)],
        out_specs=pl.BlockSpec((1, DIFF_HEADS, t, t), lambda i: (i, 0, 0, 0)),
        out_shape=jax.ShapeDtypeStruct((2, DIFF_HEADS, t, t), jnp.float32),
        compiler_params=_params("arbitrary"),
        name="t5_bias_tiles",
    )(t5_table.astype(jnp.float32))


def _mem_kv_kernel(mem_ref, g_ref, wk_ref, wvt_ref, ones_ref, gk_ref, km_ref, vmt_ref):
    x = mem_ref[0]
    h = _bf16(x * _rms_scale(x, D_MODEL) * g_ref[0])
    k = _dot(h, wk_ref[0])
    for hd in range(MEM_HEADS):
        sl = slice(hd * MEM_HEAD_DIM, (hd + 1) * MEM_HEAD_DIM)
        kh = k[:, sl]
        km_ref[0, 0, :, sl] = _bf16(kh * _rms_scale(kh, MEM_HEAD_DIM) * gk_ref[0])
    vmt_ref[0, 0] = _bf16(_dot_nt(wvt_ref[0], h) + ones_ref[...])


def _mem_kv(mem, g_mem, w_mem_kv, g_mem_k):
    b, m, _ = mem.shape
    width = MEM_HEADS * MEM_HEAD_DIM
    rows = MEM_HEADS * MEM_VROWS
    w_k = _bf16(w_mem_kv[:, :, :width])
    w_vt = _bf16(_transposed_value_weight(
        w_mem_kv[:, :, width:].reshape(DEPTH, D_MODEL, MEM_HEADS, MEM_HEAD_DIM), MEM_VROWS))
    ones = _ones_rows(MEM_HEADS, MEM_VROWS, MEM_HEAD_DIM)
    layer = lambda l, i: (l, 0, 0)
    return pl.pallas_call(
        _mem_kv_kernel,
        grid=(DEPTH, b),
        in_specs=[
            pl.BlockSpec((1, m, D_MODEL), lambda l, i: (i, 0, 0)),
            pl.BlockSpec((1, 1, D_MODEL), layer),
            pl.BlockSpec((1, D_MODEL, width), layer),
            pl.BlockSpec((1, rows, D_MODEL), layer),
            _const_spec(ones.shape),
            pl.BlockSpec((1, 1, MEM_HEAD_DIM), layer),
        ],
        out_specs=[pl.BlockSpec((1, 1, m, width), lambda l, i: (l, i, 0, 0)),
                   pl.BlockSpec((1, 1, rows, m), lambda l, i: (l, i, 0, 0))],
        out_shape=[jax.ShapeDtypeStruct((DEPTH, b, m, width), jnp.bfloat16),
                   jax.ShapeDtypeStruct((DEPTH, b, rows, m), jnp.bfloat16)],
        compiler_params=_params("arbitrary", "arbitrary"),
        name="mem_kv",
    )(mem, g_mem.reshape(DEPTH, 1, D_MODEL), w_k, w_vt, ones, g_mem_k.reshape(DEPTH, 1, MEM_HEAD_DIM))


def _rope_rows(x1, x2, cos_t, sin_t):
    return x1 * cos_t - x2 * sin_t, x2 * cos_t + x1 * sin_t


def _proj_kernel(x_ref, gmix_ref, wt_ref, gcq_ref, wuqt_ref, gckv_ref, wukt_ref, wuvt_ref, vones_ref,
                 dvones_ref, gq_ref, gk_ref, cos_ref, sin_ref, gdq_ref, gdk_ref, gmq_ref, km_ref, vmt_ref,
                 qt_ref, k_ref, vt_ref, dq0t_ref, dq1t_ref, dk_ref, dvt_ref, oc_ref, nrm_ref):
    x = x_ref[...]
    tm = x.shape[0]
    h = _bf16(x * _rms_scale(x, D_MODEL) * gmix_ref[...])
    zt = _dot_nt(wt_ref[...], h)
    cos_t, sin_t = cos_ref[...], sin_ref[...]

    c_q = zt[_R_CQ:_R_CKV]
    cq_n = _bf16(c_q * _col_scale(_sumsq(c_q), Q_LORA) * gcq_ref[...])
    qt_raw = _dot(wuqt_ref[...], cq_n)
    c_kv = zt[_R_CKV:_R_KR]
    ckv_n = _bf16(c_kv * _col_scale(_sumsq(c_kv), KV_LORA) * gckv_ref[...])
    k_nope = _dot(wukt_ref[...], ckv_n)
    vt_ref[...] = _bf16(_dot(wuvt_ref[...], ckv_n) + vones_ref[...])
    gq, gk = gq_ref[...], gk_ref[...]
    k_r = zt[_R_KR:_R_DQ]
    kr_ss = _sumsq(k_r)
    kr_g = k_r * gk[MLA_NOPE:]
    kr1, kr2 = _rope_rows(kr_g[:ROPE_HALF], kr_g[ROPE_HALF:], cos_t, sin_t)
    zero_pad = jnp.zeros((LANES - MLA_QK, tm), jnp.float32)
    for hd in range(MLA_HEADS):
        rows = slice(hd * LANES, (hd + 1) * LANES)
        q = qt_raw[hd * MLA_QK:(hd + 1) * MLA_QK]
        qn = q * _col_scale(_sumsq(q), MLA_QK) * gq
        q1, q2 = _rope_rows(qn[MLA_NOPE:MLA_NOPE + ROPE_HALF], qn[MLA_NOPE + ROPE_HALF:], cos_t, sin_t)
        qt_ref[rows, :] = _bf16(jnp.concatenate([qn[:MLA_NOPE], q1, q2, zero_pad], axis=0))
        nrm_ref[_N_MLA_Q + hd:_N_MLA_Q + hd + 1, :] = jnp.sqrt(_sumsq(qn))
        kn = k_nope[hd * MLA_NOPE:(hd + 1) * MLA_NOPE]
        rs = _col_scale(_sumsq(kn) + kr_ss, MLA_QK)
        kt = jnp.concatenate([kn * rs * gk[:MLA_NOPE], kr1 * rs, kr2 * rs, zero_pad], axis=0)
        k_ref[:, rows] = _bf16(kt.T)
        nrm_ref[_N_MLA_K + hd:_N_MLA_K + hd + 1, :] = jnp.sqrt(_sumsq(kt))

    gdq, gdk = gdq_ref[...], gdk_ref[...]
    zero_half = jnp.zeros((DIFF_QK, tm), jnp.float32)

    def half_norm(t, g):
        return t * _col_scale(_sumsq(t), DIFF_QK) * g

    for hd in range(DIFF_HEADS):
        rows = slice(hd * LANES, (hd + 1) * LANES)
        r0 = _R_DQ + hd * LANES
        c0 = half_norm(zt[r0:r0 + DIFF_QK], gdq)
        c1 = half_norm(zt[r0 + DIFF_QK:r0 + LANES], gdq)
        dq0t_ref[rows, :] = _bf16(jnp.concatenate([c0, zero_half], axis=0))
        dq1t_ref[rows, :] = _bf16(jnp.concatenate([zero_half, c1], axis=0))
        r0 = _R_DK + hd * LANES
        k0, k1 = half_norm(zt[r0:r0 + DIFF_QK], gdk), half_norm(zt[r0 + DIFF_QK:r0 + LANES], gdk)
        dk_ref[:, rows] = _bf16(jnp.concatenate([k0, k1], axis=0).T)
        for c, (qc, kc) in enumerate(((c0, k0), (c1, k1))):
            nrm_ref[_N_DIFF_Q + 2 * hd + c:_N_DIFF_Q + 2 * hd + c + 1, :] = jnp.sqrt(_sumsq(qc))
            nrm_ref[_N_DIFF_K + 2 * hd + c:_N_DIFF_K + 2 * hd + c + 1, :] = jnp.sqrt(_sumsq(kc))
    dvt_ref[...] = _bf16(zt[_R_DV:_R_MQ] + dvones_ref[...])

    gmq = gmq_ref[...]
    for hd in range(MEM_HEADS):
        sl = slice(hd * MEM_HEAD_DIM, (hd + 1) * MEM_HEAD_DIM)
        mq = zt[_R_MQ + hd * MEM_HEAD_DIM:_R_MQ + (hd + 1) * MEM_HEAD_DIM]
        mq = _bf16(mq * _col_scale(_sumsq(mq), MEM_HEAD_DIM) * gmq)
        s = _dot(km_ref[0, 0, :, sl], mq)
        p = _bf16(jnp.exp(s - jnp.max(s, axis=0, keepdims=True)))
        o = _dot(vmt_ref[0, 0, hd * MEM_VROWS:(hd + 1) * MEM_VROWS, :], p)
        o = o[:MEM_HEAD_DIM] / o[MEM_HEAD_DIM:MEM_HEAD_DIM + 1]
        oc_ref[:, sl] = _bf16(o.T)


def _proj(layer, x2, w, tabs, km, vmt, seq):
    t = x2.shape[0]
    tm = TOKEN_TILE
    steps_per_batch = seq // tm
    tok = lambda c: pl.BlockSpec((tm, c), lambda i: (i, 0))
    tokt = lambda r: pl.BlockSpec((r, tm), lambda i: (0, i))
    mem_map = lambda i: (layer, i // steps_per_batch, 0, 0)
    bf = jnp.bfloat16
    stacked = lambda name: (w[name], _layer_spec(w[name], layer))
    fixed = lambda a: (a, _const_spec(a.shape))
    consts = [stacked("g_mix"), stacked("w_t"), stacked("g_cq"), stacked("w_uqt"), stacked("g_ckv"),
              stacked("w_ukt"), stacked("w_uvt"), fixed(_ones_rows(MLA_HEADS, MLA_VROWS, MLA_V)),
              fixed(_ones_rows(DIFF_HEADS, DIFF_VROWS, DIFF_V)), stacked("g_q"), stacked("g_k")]
    gains = [stacked("g_dq"), stacked("g_dk"), stacked("g_mq")]
    wide, vt_rows, dvt_rows = MLA_HEADS * LANES, MLA_HEADS * MLA_VROWS, DIFF_HEADS * DIFF_VROWS
    outs = [(tokt(wide), (wide, t)), (tok(wide), (t, wide)), (tokt(vt_rows), (vt_rows, t)),
            (tokt(BRANCH_W), (BRANCH_W, t)), (tokt(BRANCH_W), (BRANCH_W, t)), (tok(BRANCH_W), (t, BRANCH_W)),
            (tokt(dvt_rows), (dvt_rows, t)), (tok(BRANCH_W), (t, BRANCH_W))]
    return pl.pallas_call(
        _proj_kernel,
        grid=(t // tm,),
        in_specs=([tok(D_MODEL)] + [spec for _, spec in consts] + [tokt(ROPE_HALF)] * 2
                  + [spec for _, spec in gains]
                  + [pl.BlockSpec((1, 1) + km.shape[2:], mem_map), pl.BlockSpec((1, 1) + vmt.shape[2:], mem_map)]),
        out_specs=[spec for spec, _ in outs] + [tokt(NORM_ROWS)],
        out_shape=([jax.ShapeDtypeStruct(shape, bf) for _, shape in outs]
                   + [jax.ShapeDtypeStruct((NORM_ROWS, t), jnp.float32)]),
        compiler_params=_params("arbitrary"),
        name="proj",
    )(x2, *[a for a, _ in consts], *tabs, *[a for a, _ in gains], km, vmt)


def _chunk_mask_t(shape):
    krow = lax.broadcasted_iota(jnp.int32, shape, 0)
    qcol = lax.broadcasted_iota(jnp.int32, shape, 1)
    return (krow // CHUNK) <= (qcol // CHUNK)


def _flash_step(n, score_fn, vt_fn, m_ref, acc_ref, online):
    def scores_and_max(i):
        s = score_fn(i)
        if not online:
            return s, m_ref[i], None
        m_old = m_ref[i]
        m_new = jnp.maximum(m_old, jnp.max(s, axis=0, keepdims=True))
        m_ref[i] = m_new
        return s, m_new, jnp.exp2(m_old - m_new)

    def exponentials(a):
        s, m_new, alpha = a
        return _bf16(jnp.exp2(s - m_new)), alpha

    def value_product(i, b):
        p, alpha = b
        acc = acc_ref[i] if alpha is None else alpha * acc_ref[i]
        acc_ref[i] = acc + _dot(vt_fn(i), p)

    stage_a, stage_b = {}, {}
    for it in range(n + 2):
        if it < n:
            stage_a[it] = scores_and_max(it)
        if 0 <= it - 1 < n:
            stage_b[it - 1] = exponentials(stage_a.pop(it - 1))
        if 0 <= it - 2 < n:
            value_product(it - 2, stage_b.pop(it - 2))


def _softmax_modes(safe_ref, m_ref, acc_ref, bound_fn, run):
    acc_ref[...] = jnp.zeros(acc_ref.shape, jnp.float32)

    @pl.when(safe_ref[0] == 1)
    def _():
        for i in range(m_ref.shape[0]):
            m_ref[i] = bound_fn(i)
        run(False)

    @pl.when(safe_ref[0] != 1)
    def _():
        m_ref[...] = jnp.full(m_ref.shape, NEG, jnp.float32)
        run(True)


def _key_tile(start_tile, n_tiles, ta):
    return pl.ds(pl.multiple_of(start_tile * ta, ta), n_tiles * ta)


def _for_unmasked_tiles(n, tile):
    def body(i, carry):
        tile(i * FAR_TILES, FAR_TILES)
        return carry

    n = jnp.maximum(n, 0)
    lax.fori_loop(0, n // FAR_TILES, body, 0)
    for r in range(1, FAR_TILES):
        @pl.when(n % FAR_TILES >= r)
        def _():
            tile(n - n % FAR_TILES + (r - 1), 1)


def _mla_attn_kernel(safe_ref, kmax_ref, qt_ref, qn_ref, k_ref, vt_ref, o_ref, m_ref, acc_ref):
    b, qi = pl.program_id(0), pl.program_id(1)
    ta = qt_ref.shape[1]

    def run(online):
        def tile(start, n_tiles, masked=False):
            keys = _key_tile(start, n_tiles, ta)
            mask = _chunk_mask_t((ta, ta)) if masked else None

            def score(hd):
                sl = slice(hd * LANES, (hd + 1) * LANES)
                s = _dot(k_ref[keys, sl], qt_ref[sl, :])
                return jnp.where(mask, s, NEG) if masked else s

            _flash_step(MLA_HEADS, score,
                        lambda hd: vt_ref[hd * MLA_VROWS:(hd + 1) * MLA_VROWS, keys], m_ref, acc_ref, online)

        _for_unmasked_tiles(qi, tile)
        tile(qi, 1, masked=True)

    def bound(hd):
        return qn_ref[hd:hd + 1, :] * (kmax_ref[b * MLA_HEADS + hd] * BOUND_MARGIN)

    _softmax_modes(safe_ref, m_ref, acc_ref, bound, run)
    outs = []
    for hd in range(MLA_HEADS):
        a = acc_ref[hd]
        outs.append(a[:MLA_V] / a[MLA_V:MLA_V + 1])
    o_ref[...] = _bf16(jnp.concatenate(outs, axis=0).T)


def _norm_rows_spec(first_row, n_rows, ta, nb):
    assert n_rows == F32_SUBLANES and first_row % n_rows == 0
    return pl.BlockSpec((n_rows, ta), lambda b, qi: (first_row // n_rows, b * nb + qi))


def _mla_attn(safe, kmax, qt, nrm, k, vt, batch, seq):
    ta = ATTN_TILE
    nb = seq // ta
    smem = pl.BlockSpec(memory_space=pltpu.SMEM)
    return pl.pallas_call(
        _mla_attn_kernel,
        grid=(batch, nb),
        in_specs=[smem, smem,
                  pl.BlockSpec((MLA_HEADS * LANES, ta), lambda b, qi: (0, b * nb + qi)),
                  _norm_rows_spec(_N_MLA_Q, MLA_HEADS, ta, nb),
                  pl.BlockSpec((seq, MLA_HEADS * LANES), lambda b, qi: (b, 0)),
                  pl.BlockSpec((MLA_HEADS * MLA_VROWS, seq), lambda b, qi: (0, b))],
        out_specs=pl.BlockSpec((ta, BRANCH_W), lambda b, qi: (b * nb + qi, 0)),
        out_shape=jax.ShapeDtypeStruct((batch * seq, BRANCH_W), jnp.bfloat16),
        scratch_shapes=[pltpu.VMEM((MLA_HEADS, 1, ta), jnp.float32),
                        pltpu.VMEM((MLA_HEADS, MLA_VROWS, ta), jnp.float32)],
        compiler_params=_params("arbitrary", "arbitrary"),
        name="mla_attn",
    )(safe, kmax, qt, nrm, k, vt)


def _diff_attn_kernel(safe_ref, kmax_ref, bias_hi_ref, lam_ref, q0t_ref, q1t_ref, qn_ref, k_ref, vt_ref,
                      bias_ref, gout_ref, o_ref, m_ref, acc_ref, *, layer, out_scale):
    b, qi = pl.program_id(0), pl.program_id(1)
    ta = q0t_ref.shape[1]

    def run(online):
        def tile(start, n_tiles, bias_tile=None, masked=False):
            keys = _key_tile(start, n_tiles, ta)
            mask = _chunk_mask_t((ta, ta)) if masked else None

            def score(idx):
                hd, qt_ref = idx // 2, (q0t_ref, q1t_ref)[idx % 2]
                sl = slice(hd * LANES, (hd + 1) * LANES)
                s = _dot(k_ref[keys, sl], qt_ref[sl, :])
                if bias_tile is not None:
                    s = s + bias_ref[bias_tile, hd]
                return jnp.where(mask, s, NEG) if masked else s

            _flash_step(2 * DIFF_HEADS, score,
                        lambda idx: vt_ref[(idx // 2) * DIFF_VROWS:(idx // 2 + 1) * DIFF_VROWS, keys],
                        m_ref, acc_ref, online)

        _for_unmasked_tiles(qi - 1, tile)

        @pl.when(qi > 0)
        def _():
            tile(qi - 1, 1, bias_tile=1)

        tile(qi, 1, bias_tile=0, masked=True)

    def bound(idx):
        return (qn_ref[idx:idx + 1, :] * (kmax_ref[b * 2 * DIFF_HEADS + idx] * BOUND_MARGIN)
                + bias_hi_ref[idx // 2])

    _softmax_modes(safe_ref, m_ref, acc_ref, bound, run)
    lam = lam_ref[layer]
    g = gout_ref[...] * out_scale
    outs = []
    for hd in range(DIFF_HEADS):
        a0, a1 = acc_ref[2 * hd], acc_ref[2 * hd + 1]
        o = a0[:DIFF_V] / a0[DIFF_V:DIFF_V + 1] - lam * (a1[:DIFF_V] / a1[DIFF_V:DIFF_V + 1])
        outs.append(o * _col_scale(_sumsq(o), DIFF_V) * g)
    o_ref[...] = _bf16(jnp.concatenate(outs, axis=0).T)


def _diff_attn(layer, safe, kmax, bias_hi, lam, q0t, q1t, nrm, k, vt, bias, g_out, out_scale, batch, seq):
    ta = ATTN_TILE
    nb = seq // ta
    qt_spec = pl.BlockSpec((BRANCH_W, ta), lambda b, qi: (0, b * nb + qi))
    smem = pl.BlockSpec(memory_space=pltpu.SMEM)
    return pl.pallas_call(
        functools.partial(_diff_attn_kernel, layer=layer, out_scale=out_scale),
        grid=(batch, nb),
        in_specs=[smem, smem, smem, smem, qt_spec, qt_spec,
                  _norm_rows_spec(_N_DIFF_Q, 2 * DIFF_HEADS, ta, nb),
                  pl.BlockSpec((seq, BRANCH_W), lambda b, qi: (b, 0)),
                  pl.BlockSpec((DIFF_HEADS * DIFF_VROWS, seq), lambda b, qi: (0, b)),
                  _single_spec(bias), _layer_spec(g_out, layer, single=True)],
        out_specs=pl.BlockSpec((ta, BRANCH_W), lambda b, qi: (b * nb + qi, 0)),
        out_shape=jax.ShapeDtypeStruct((batch * seq, BRANCH_W), jnp.bfloat16),
        scratch_shapes=[pltpu.VMEM((2 * DIFF_HEADS, 1, ta), jnp.float32),
                        pltpu.VMEM((2 * DIFF_HEADS, DIFF_VROWS, ta), jnp.float32)],
        compiler_params=_params("arbitrary", "arbitrary"),
        name="diff_attn",
    )(safe, kmax, bias_hi, lam, q0t, q1t, nrm, k, vt, bias, g_out)


def _merge_mlp_kernel(x_ref, oa_ref, ob_ref, oc_ref, gmix_ref, wg_ref, wb_ref, wout_ref,
                      gmlp_ref, w1_ref, w2_ref, out_ref):
    x = x_ref[...]
    h = _bf16(x * _rms_scale(x, D_MODEL) * gmix_ref[...])
    y = None
    for n, o_ref in enumerate((oa_ref, ob_ref, oc_ref)):
        gate = 1.0 / (1.0 + jnp.exp(-_dot(h, wg_ref[:, n * D_MODEL:(n + 1) * D_MODEL])))
        term = gate * _dot(o_ref[...], wb_ref[n])
        y = term if y is None else y + term
    x = x + _dot(_bf16(y), wout_ref[...])
    h2 = _bf16(x * _rms_scale(x, D_MODEL) * gmlp_ref[...])
    for c in range(D_FF // FF_CHUNK):
        f = jnp.maximum(_dot(h2, w1_ref[:, c * FF_CHUNK:(c + 1) * FF_CHUNK]), 0.0)
        x = x + _dot(_bf16(f * f), w2_ref[c * FF_CHUNK:(c + 1) * FF_CHUNK, :])
    out_ref[...] = x


def _merge_mlp(layer, x2, o_a, o_b, o_c, w):
    t = x2.shape[0]
    tm = TOKEN_TILE
    tok = lambda c: pl.BlockSpec((tm, c), lambda i: (i, 0))
    consts = [w["g_mix"], w["w_gate"], w["w_branch"], w["w_out"], w["g_mlp"], w["w_ff1"], w["w_ff2"]]
    return pl.pallas_call(
        _merge_mlp_kernel,
        grid=(t // tm,),
        in_specs=([tok(D_MODEL)] + [tok(BRANCH_W)] * 3
                  + [_layer_spec(c, layer, single=True) for c in consts]),
        out_specs=tok(D_MODEL),
        out_shape=jax.ShapeDtypeStruct((t, D_MODEL), jnp.float32),
        compiler_params=_params("arbitrary"),
        name="merge_mlp",
    )(x2, o_a, o_b, o_c, *consts)


def _transposed_value_weight(w, rows):
    depth, n_in, heads, width = w.shape
    wt = jnp.pad(jnp.transpose(w, (0, 2, 3, 1)), ((0, 0), (0, 0), (0, rows - width), (0, 0)))
    return wt.reshape(depth, heads * rows, n_in)


def _ones_rows(heads, rows, at):
    col = np.zeros((heads, rows, 1), np.float32)
    col[:, at] = 1.0
    return jnp.asarray(col.reshape(heads * rows, 1))


def _layer_weights(w_in, g_mix, g_cq, w_uq, g_ckv, w_ukv, g_mla_q, g_mla_k, g_diff_q, g_diff_k,
                   g_mem_q, w_branch, w_out, g_mlp, w_ff1, w_ff2):
    splits = np.cumsum([Q_LORA, KV_LORA, MLA_ROPE, DIFF_HEADS * 2 * DIFF_QK, DIFF_HEADS * 2 * DIFF_QK,
                        DIFF_HEADS * DIFF_V, MEM_HEADS * MEM_HEAD_DIM]).tolist()
    c_dv, c_mq, c_gate = splits[4], splits[5], splits[6]
    t_in = jnp.swapaxes(w_in[:, :, :c_gate], 1, 2)
    w_dvt = _transposed_value_weight(
        w_in[:, :, c_dv:c_mq].reshape(DEPTH, D_MODEL, DIFF_HEADS, DIFF_V), DIFF_VROWS)
    w_t = jnp.concatenate([t_in[:, :c_dv], w_dvt, t_in[:, c_mq:]], axis=1)
    assert w_t.shape[1] == _R_END
    w_uqt = _transposed_value_weight(w_uq.reshape(DEPTH, Q_LORA, MLA_HEADS, MLA_QK), MLA_QK)
    w_ukv = w_ukv.reshape(DEPTH, KV_LORA, MLA_HEADS, MLA_NOPE + MLA_V)
    w_ukt = _transposed_value_weight(w_ukv[..., :MLA_NOPE], MLA_NOPE)
    w_uvt = _transposed_value_weight(w_ukv[..., MLA_NOPE:], MLA_VROWS)
    row = lambda g: g[:, None, :]
    col = lambda g: g[:, :, None]
    return {
        "g_mix": row(g_mix), "w_t": _bf16(w_t), "g_cq": col(g_cq), "w_uqt": _bf16(w_uqt),
        "g_ckv": col(g_ckv), "w_ukt": _bf16(w_ukt), "w_uvt": _bf16(w_uvt),
        "g_q": col(g_mla_q * (MLA_QK ** -0.5 * LOG2E)), "g_k": col(g_mla_k),
        "g_dq": col(g_diff_q * (DIFF_QK ** -0.5 * LOG2E)), "g_dk": col(g_diff_k),
        "g_mq": col(g_mem_q * MEM_HEAD_DIM ** -0.5),
        "w_gate": _bf16(w_in[:, :, c_gate:]), "w_branch": _bf16(w_branch), "w_out": _bf16(w_out),
        "g_mlp": row(g_mlp), "w_ff1": _bf16(w_ff1), "w_ff2": _bf16(w_ff2),
    }


def _logit_bounds(nrm, q_row, k_row, n, batch, bias_range):
    by_row = nrm.reshape(NORM_ROWS, batch, -1)
    qmax = jnp.max(by_row[q_row:q_row + n], axis=-1)
    kmax = jnp.max(by_row[k_row:k_row + n], axis=-1)
    spread = 2.0 * BOUND_MARGIN * jnp.max(qmax * kmax) + bias_range
    safe = (spread <= SAFE_LOGIT_SPREAD).astype(jnp.int32).reshape(1)
    return safe, kmax.T.reshape(-1)


def kernel(x, mem, positions, t5_table, g_mix, g_mem, w_in, g_cq, w_uq, g_ckv, w_ukv, g_mla_q, g_mla_k, g_diff_q, g_diff_k, lam_q1, lam_k1, lam_q2, lam_k2, g_diff_out, w_mem_kv, g_mem_q, g_mem_k, w_branch, w_out, g_mlp, w_ff1, w_ff2):
    b, s, _ = x.shape
    assert s % ATTN_TILE == 0 and s % TOKEN_TILE == 0 and (b * s) % TABLE_TILE == 0
    tabs = _rope_tables(positions)
    bias = _t5_bias_tiles(t5_table)
    km, vmt = _mem_kv(mem, g_mem, w_mem_kv, g_mem_k)
    weights = _layer_weights(w_in, g_mix, g_cq, w_uq, g_ckv, w_ukv, g_mla_q, g_mla_k, g_diff_q,
                             g_diff_k, g_mem_q, w_branch, w_out, g_mlp, w_ff1, w_ff2)
    f32 = jnp.float32
    lam_init = [0.8 - 0.6 * math.exp(-0.3 * l) for l in range(DEPTH)]
    lam = (jnp.exp(jnp.sum(lam_q1.astype(f32) * lam_k1.astype(f32), axis=-1))
           - jnp.exp(jnp.sum(lam_q2.astype(f32) * lam_k2.astype(f32), axis=-1))
           + jnp.asarray(lam_init, f32))
    x2 = x.reshape(b * s, D_MODEL)
    g_out = g_diff_out[:, :, None]
    bias_hi = jnp.max(bias, axis=(0, 2, 3))
    bias_range = jnp.max(bias_hi - jnp.min(bias, axis=(0, 2, 3)))
    for l in range(DEPTH):
        qt, k, vt, dq0t, dq1t, dk, dvt, o_c, nrm = _proj(l, x2, weights, tabs, km, vmt, s)
        safe_a, kmax_a = _logit_bounds(nrm, _N_MLA_Q, _N_MLA_K, MLA_HEADS, b, 0.0)
        o_a = _mla_attn(safe_a, kmax_a, qt, nrm, k, vt, b, s)
        safe_b, kmax_b = _logit_bounds(nrm, _N_DIFF_Q, _N_DIFF_K, 2 * DIFF_HEADS, b, bias_range)
        o_b = _diff_attn(l, safe_b, kmax_b, bias_hi, lam, dq0t, dq1t, nrm, dk, dvt, bias, g_out,
                         1.0 - lam_init[l], b, s)
        x2 = _merge_mlp(l, x2, o_a, o_b, o_c, weights)
    return x2.reshape(b, s, D_MODEL)
```

```python
import functools
import math

import jax
import jax.numpy as jnp
import numpy as np
from jax import lax
from jax.experimental import pallas as pl
from jax.experimental.pallas import tpu as pltpu

D_MODEL = 1024
DEPTH = 4
CHUNK = 64
MLA_HEADS = 8
MLA_NOPE = 64
MLA_ROPE = 32
MLA_V = 64
MLA_QK = MLA_NOPE + MLA_ROPE
Q_LORA = 384
KV_LORA = 256
ROPE_BASE = 10000.0
DIFF_HEADS = 4
DIFF_QK = 64
DIFF_V = 2 * DIFF_QK
MEM_HEADS = 4
MEM_HEAD_DIM = 128
BRANCH_W = 512
N_BRANCH = 3
D_FF = 4 * D_MODEL
T5_BUCKETS = 32
T5_MAX_DIST = 128
EPS = 1e-6
NEG = -1e30

LANES = 128
BF16_SUBLANES = 16
V7X_VMEM_BYTES = 64 * 1024 * 1024
VMEM_LIMIT = V7X_VMEM_BYTES * 7 // 8

TOKEN_TILE = 512
ATTN_TILE = 512
FAR_TILES = 2
TABLE_TILE = 4096
FF_CHUNK = 1024
ROPE_HALF = MLA_ROPE // 2


def _round_up(n, m):
    return -(-n // m) * m


MLA_VROWS = _round_up(MLA_V + 1, BF16_SUBLANES)
DIFF_VROWS = _round_up(DIFF_V + 1, BF16_SUBLANES)
MEM_VROWS = _round_up(MEM_HEAD_DIM + 1, BF16_SUBLANES)
LOG2E = math.log2(math.e)

SAFE_LOGIT_SPREAD = 120.0
BOUND_MARGIN = 1.02

_R_CQ = 0
_R_CKV = _R_CQ + Q_LORA
_R_KR = _R_CKV + KV_LORA
_R_DQ = _R_KR + MLA_ROPE
_R_DK = _R_DQ + DIFF_HEADS * LANES
_R_DV = _R_DK + DIFF_HEADS * LANES
_R_MQ = _R_DV + DIFF_HEADS * DIFF_VROWS
_R_END = _R_MQ + MEM_HEADS * MEM_HEAD_DIM


def _bf16(x):
    return x.astype(jnp.bfloat16)


def _dot(a, b):
    return jnp.dot(a, b, preferred_element_type=jnp.float32)


def _dot_nt(a, b):
    return lax.dot_general(a, b, (((1,), (1,)), ((), ())), preferred_element_type=jnp.float32)


def _rms_scale(x, n):
    return lax.rsqrt(jnp.sum(x * x, axis=-1, keepdims=True) * (1.0 / n) + EPS)


def _sumsq(x):
    return jnp.sum(x * x, axis=0, keepdims=True)


def _col_scale(sumsq, n):
    return lax.rsqrt(sumsq * (1.0 / n) + EPS)


def _const_spec(shape):
    return pl.BlockSpec(shape, lambda *_: (0,) * len(shape))


def _single_spec(a):
    return pl.BlockSpec(a.shape, lambda *_: (0,) * a.ndim, pipeline_mode=pl.Buffered(1))


def _layer_spec(a, layer, single=False):
    return pl.BlockSpec((None,) + a.shape[1:], lambda *_: (layer,) + (0,) * (a.ndim - 1),
                        pipeline_mode=pl.Buffered(1) if single else None)


def _params(*sem):
    return pltpu.CompilerParams(dimension_semantics=sem, vmem_limit_bytes=VMEM_LIMIT)


def _rope_tables_kernel(pos_ref, inv_ref, cos_ref, sin_ref):
    ang = inv_ref[...] * pos_ref[...].astype(jnp.float32)
    cos_ref[...] = jnp.cos(ang)
    sin_ref[...] = jnp.sin(ang)


def _rope_tables(positions):
    t = positions.size
    inv = jnp.power(jnp.float32(ROPE_BASE), -jnp.arange(ROPE_HALF, dtype=jnp.float32) / ROPE_HALF)
    out = jax.ShapeDtypeStruct((ROPE_HALF, t), jnp.float32)
    return pl.pallas_call(
        _rope_tables_kernel,
        grid=(t // TABLE_TILE,),
        in_specs=[pl.BlockSpec((1, TABLE_TILE), lambda i: (0, i)), _const_spec((ROPE_HALF, 1))],
        out_specs=[pl.BlockSpec((ROPE_HALF, TABLE_TILE), lambda i: (0, i))] * 2,
        out_shape=[out] * 2,
        compiler_params=_params("arbitrary"),
        name="rope_tables",
    )(positions.reshape(1, t), inv.reshape(ROPE_HALF, 1))


def _t5_large_thresholds():
    n = T5_BUCKETS // 2
    max_exact = n // 2
    assert T5_MAX_DIST == 16 * max_exact and n - max_exact == 8
    out = []
    for j in range(1, n - max_exact):
        a = max_exact
        while a * a < max_exact * max_exact * 2 ** j:
            a += 1
        out.append(a)
    return out


def _t5_bias_kernel(table_ref, bias_ref):
    t = bias_ref.shape[-1]
    off = pl.program_id(0) * t
    row = lax.broadcasted_iota(jnp.int32, (t, t), 0)
    col = lax.broadcasted_iota(jnp.int32, (t, t), 1)
    rel = row - col - off
    n = T5_BUCKETS // 2
    max_exact = n // 2
    a = jnp.abs(rel)
    large = jnp.full((t, t), max_exact, jnp.int32)
    for thr in _t5_large_thresholds():
        large = large + (a >= thr).astype(jnp.int32)
    bucket = jnp.where(rel > 0, n, 0) + jnp.where(a < max_exact, a, large)
    far_bucket = n - 1
    for h in range(DIFF_HEADS):
        val = jnp.full((t, t), table_ref[0, h], jnp.float32)
        for b in range(1, T5_BUCKETS):
            val = jnp.where(bucket == b, table_ref[b, h], val)
        bias_ref[0, h] = (val - table_ref[far_bucket, h]) * LOG2E


def _t5_bias_tiles(t5_table):
    t = ATTN_TILE
    assert t >= _t5_large_thresholds()[-1]
    return pl.pallas_call(
        _t5_bias_kernel,
        grid=(2,),
        in_specs=[pl.BlockSpec(memory_space=pltpu.SMEM)],
        out_specs=pl.BlockSpec((1, DIFF_HEADS, t, t), lambda i: (i, 0, 0, 0)),
        out_shape=jax.ShapeDtypeStruct((2, DIFF_HEADS, t, t), jnp.float32),
        compiler_params=_params("arbitrary"),
        name="t5_bias_tiles",
    )(t5_table.astype(jnp.float32))


def _mem_kv_kernel(mem_ref, g_ref, wk_ref, wvt_ref, ones_ref, gk_ref, km_ref, vmt_ref):
    x = mem_ref[0]
    h = _bf16(x * _rms_scale(x, D_MODEL) * g_ref[0])
    k = _dot(h, wk_ref[0])
    for hd in range(MEM_HEADS):
        sl = slice(hd * MEM_HEAD_DIM, (hd + 1) * MEM_HEAD_DIM)
        kh = k[:, sl]
        km_ref[0, 0, :, sl] = _bf16(kh * _rms_scale(kh, MEM_HEAD_DIM) * gk_ref[0])
    vmt_ref[0, 0] = _bf16(_dot_nt(wvt_ref[0], h) + ones_ref[...])


def _mem_kv(mem, g_mem, w_mem_kv, g_mem_k):
    b, m, _ = mem.shape
    width = MEM_HEADS * MEM_HEAD_DIM
    rows = MEM_HEADS * MEM_VROWS
    w_k = _bf16(w_mem_kv[:, :, :width])
    w_vt = _bf16(_transposed_value_weight(
        w_mem_kv[:, :, width:].reshape(DEPTH, D_MODEL, MEM_HEADS, MEM_HEAD_DIM), MEM_VROWS))
    ones = _ones_rows(MEM_HEADS, MEM_VROWS, MEM_HEAD_DIM)
    layer = lambda l, i: (l, 0, 0)
    return pl.pallas_call(
        _mem_kv_kernel,
        grid=(DEPTH, b),
        in_specs=[
            pl.BlockSpec((1, m, D_MODEL), lambda l, i: (i, 0, 0)),
            pl.BlockSpec((1, 1, D_MODEL), layer),
            pl.BlockSpec((1, D_MODEL, width), layer),
            pl.BlockSpec((1, rows, D_MODEL), layer),
            _const_spec(ones.shape),
            pl.BlockSpec((1, 1, MEM_HEAD_DIM), layer),
        ],
        out_specs=[pl.BlockSpec((1, 1, m, width), lambda l, i: (l, i, 0, 0)),
                   pl.BlockSpec((1, 1, rows, m), lambda l, i: (l, i, 0, 0))],
        out_shape=[jax.ShapeDtypeStruct((DEPTH, b, m, width), jnp.bfloat16),
                   jax.ShapeDtypeStruct((DEPTH, b, rows, m), jnp.bfloat16)],
        compiler_params=_params("arbitrary", "arbitrary"),
        name="mem_kv",
    )(mem, g_mem.reshape(DEPTH, 1, D_MODEL), w_k, w_vt, ones, g_mem_k.reshape(DEPTH, 1, MEM_HEAD_DIM))


def _rope_rows(x1, x2, cos_t, sin_t):
    return x1 * cos_t - x2 * sin_t, x2 * cos_t + x1 * sin_t


def _proj_kernel(x_ref, gmix_ref, wt_ref, gcq_ref, wuqt_ref, gckv_ref, wukt_ref, wuvt_ref, vones_ref,
                 dvones_ref, gq_ref, gk_ref, qpad_ref, kpad_ref, cos_ref, sin_ref, gdq_ref, gdk_ref, gmq_ref,
                 km_ref, vmt_ref,
                 qt_ref, k_ref, vt_ref, dq0t_ref, dq1t_ref, dk_ref, dvt_ref, oc_ref):
    x = x_ref[...]
    tm = x.shape[0]
    h = _bf16(x * _rms_scale(x, D_MODEL) * gmix_ref[...])
    zt = _dot_nt(wt_ref[...], h)
    cos_t, sin_t = cos_ref[...], sin_ref[...]

    c_q = zt[_R_CQ:_R_CKV]
    cq_n = _bf16(c_q * _col_scale(_sumsq(c_q), Q_LORA) * gcq_ref[...])
    qt_raw = _dot(wuqt_ref[...], cq_n)
    c_kv = zt[_R_CKV:_R_KR]
    ckv_n = _bf16(c_kv * _col_scale(_sumsq(c_kv), KV_LORA) * gckv_ref[...])
    k_nope = _dot(wukt_ref[...], ckv_n)
    vt_ref[...] = _bf16(_dot(wuvt_ref[...], ckv_n) + vones_ref[...])
    gq, gk = gq_ref[...], gk_ref[...]
    k_r = zt[_R_KR:_R_DQ]
    kr_ss = _sumsq(k_r)
    kr_g = k_r * gk[MLA_NOPE:]
    kr1, kr2 = _rope_rows(kr_g[:ROPE_HALF], kr_g[ROPE_HALF:], cos_t, sin_t)
    q_pad = jnp.broadcast_to(qpad_ref[...], (LANES - MLA_QK, tm))
    k_pad = jnp.broadcast_to(kpad_ref[...], (LANES - MLA_QK, tm))
    for hd in range(MLA_HEADS):
        rows = slice(hd * LANES, (hd + 1) * LANES)
        q = qt_raw[hd * MLA_QK:(hd + 1) * MLA_QK]
        qn = q * _col_scale(_sumsq(q), MLA_QK) * gq
        q1, q2 = _rope_rows(qn[MLA_NOPE:MLA_NOPE + ROPE_HALF], qn[MLA_NOPE + ROPE_HALF:], cos_t, sin_t)
        qt_ref[rows, :] = _bf16(jnp.concatenate([qn[:MLA_NOPE], q1, q2, q_pad], axis=0))
        kn = k_nope[hd * MLA_NOPE:(hd + 1) * MLA_NOPE]
        rs = _col_scale(_sumsq(kn) + kr_ss, MLA_QK)
        kt = jnp.concatenate([kn * rs * gk[:MLA_NOPE], kr1 * rs, kr2 * rs, k_pad], axis=0)
        k_ref[:, rows] = _bf16(kt.T)

    gdq, gdk = gdq_ref[...], gdk_ref[...]
    zero_half = jnp.zeros((DIFF_QK, tm), jnp.float32)

    def half_norm(t, g):
        return t * _col_scale(_sumsq(t), DIFF_QK) * g

    for hd in range(DIFF_HEADS):
        rows = slice(hd * LANES, (hd + 1) * LANES)
        r0 = _R_DQ + hd * LANES
        c0 = half_norm(zt[r0:r0 + DIFF_QK], gdq)
        c1 = half_norm(zt[r0 + DIFF_QK:r0 + LANES], gdq)
        dq0t_ref[rows, :] = _bf16(jnp.concatenate([c0, zero_half], axis=0))
        dq1t_ref[rows, :] = _bf16(jnp.concatenate([zero_half, c1], axis=0))
        r0 = _R_DK + hd * LANES
        k0, k1 = half_norm(zt[r0:r0 + DIFF_QK], gdk), half_norm(zt[r0 + DIFF_QK:r0 + LANES], gdk)
        dk_ref[:, rows] = _bf16(jnp.concatenate([k0, k1], axis=0).T)
    dvt_ref[...] = _bf16(zt[_R_DV:_R_MQ] + dvones_ref[...])

    gmq = gmq_ref[...]
    for hd in range(MEM_HEADS):
        sl = slice(hd * MEM_HEAD_DIM, (hd + 1) * MEM_HEAD_DIM)
        mq = zt[_R_MQ + hd * MEM_HEAD_DIM:_R_MQ + (hd + 1) * MEM_HEAD_DIM]
        mq = _bf16(mq * _col_scale(_sumsq(mq), MEM_HEAD_DIM) * gmq)
        s = _dot(km_ref[0, 0, :, sl], mq)
        p = _bf16(jnp.exp(s - jnp.max(s, axis=0, keepdims=True)))
        o = _dot(vmt_ref[0, 0, hd * MEM_VROWS:(hd + 1) * MEM_VROWS, :], p)
        o = o[:MEM_HEAD_DIM] / o[MEM_HEAD_DIM:MEM_HEAD_DIM + 1]
        oc_ref[:, sl] = _bf16(o.T)


def _proj(layer, x2, w, tabs, km, vmt, seq):
    t = x2.shape[0]
    tm = TOKEN_TILE
    steps_per_batch = seq // tm
    tok = lambda c: pl.BlockSpec((tm, c), lambda i: (i, 0))
    tokt = lambda r: pl.BlockSpec((r, tm), lambda i: (0, i))
    mem_map = lambda i: (layer, i // steps_per_batch, 0, 0)
    bf = jnp.bfloat16
    stacked = lambda name: (w[name], _layer_spec(w[name], layer))
    fixed = lambda a: (a, _const_spec(a.shape))
    consts = [stacked("g_mix"), stacked("w_t"), stacked("g_cq"), stacked("w_uqt"), stacked("g_ckv"),
              stacked("w_ukt"), stacked("w_uvt"), fixed(_ones_rows(MLA_HEADS, MLA_VROWS, MLA_V)),
              fixed(_ones_rows(DIFF_HEADS, DIFF_VROWS, DIFF_V)), stacked("g_q"), stacked("g_k"),
              stacked("q_pad"), fixed(_ones_rows(1, LANES - MLA_QK, 0))]
    gains = [stacked("g_dq"), stacked("g_dk"), stacked("g_mq")]
    wide, vt_rows, dvt_rows = MLA_HEADS * LANES, MLA_HEADS * MLA_VROWS, DIFF_HEADS * DIFF_VROWS
    outs = [(tokt(wide), (wide, t)), (tok(wide), (t, wide)), (tokt(vt_rows), (vt_rows, t)),
            (tokt(BRANCH_W), (BRANCH_W, t)), (tokt(BRANCH_W), (BRANCH_W, t)), (tok(BRANCH_W), (t, BRANCH_W)),
            (tokt(dvt_rows), (dvt_rows, t)), (tok(BRANCH_W), (t, BRANCH_W))]
    return pl.pallas_call(
        _proj_kernel,
        grid=(t // tm,),
        in_specs=([tok(D_MODEL)] + [spec for _, spec in consts] + [tokt(ROPE_HALF)] * 2
                  + [spec for _, spec in gains]
                  + [pl.BlockSpec((1, 1) + km.shape[2:], mem_map), pl.BlockSpec((1, 1) + vmt.shape[2:], mem_map)]),
        out_specs=[spec for spec, _ in outs],
        out_shape=[jax.ShapeDtypeStruct(shape, bf) for _, shape in outs],
        compiler_params=_params("arbitrary"),
        name="proj",
    )(x2, *[a for a, _ in consts], *tabs, *[a for a, _ in gains], km, vmt)


def _chunk_mask_t(shape):
    krow = lax.broadcasted_iota(jnp.int32, shape, 0)
    qcol = lax.broadcasted_iota(jnp.int32, shape, 1)
    return (krow // CHUNK) <= (qcol // CHUNK)


def _flash_step(n, score_fn, vt_fn, m_ref, acc_ref, online):
    def scores_and_max(i):
        s = score_fn(i)
        if not online:
            return s, None, None
        m_old = m_ref[i]
        m_new = jnp.maximum(m_old, jnp.max(s, axis=0, keepdims=True))
        m_ref[i] = m_new
        return s, m_new, jnp.exp2(m_old - m_new)

    def exponentials(a):
        s, m_new, alpha = a
        return _bf16(jnp.exp2(s if m_new is None else s - m_new)), alpha

    def value_product(i, b):
        p, alpha = b
        acc = acc_ref[i] if alpha is None else alpha * acc_ref[i]
        acc_ref[i] = acc + _dot(vt_fn(i), p)

    stage_a, stage_b = {}, {}
    for it in range(n + 2):
        if it < n:
            stage_a[it] = scores_and_max(it)
        if 0 <= it - 1 < n:
            stage_b[it - 1] = exponentials(stage_a.pop(it - 1))
        if 0 <= it - 2 < n:
            value_product(it - 2, stage_b.pop(it - 2))


def _softmax_modes(safe, m_ref, acc_ref, run):
    acc_ref[...] = jnp.zeros(acc_ref.shape, jnp.float32)

    @pl.when(safe == 1)
    def _():
        run(False)

    @pl.when(safe != 1)
    def _():
        m_ref[...] = jnp.full(m_ref.shape, NEG, jnp.float32)
        run(True)


def _key_tile(start_tile, n_tiles, ta):
    return pl.ds(pl.multiple_of(start_tile * ta, ta), n_tiles * ta)


def _for_unmasked_tiles(n, tile):
    def body(i, carry):
        tile(i * FAR_TILES, FAR_TILES)
        return carry

    n = jnp.maximum(n, 0)
    lax.fori_loop(0, n // FAR_TILES, body, 0)
    for r in range(1, FAR_TILES):
        @pl.when(n % FAR_TILES >= r)
        def _():
            tile(n - n % FAR_TILES + (r - 1), 1)


def _mla_attn_kernel(safe_ref, qt_ref, k_ref, vt_ref, o_ref, m_ref, acc_ref, *, layer):
    qi = pl.program_id(1)
    ta = qt_ref.shape[1]

    def run(online):
        def tile(start, n_tiles, masked=False):
            keys = _key_tile(start, n_tiles, ta)
            mask = _chunk_mask_t((ta, ta)) if masked else None

            def score(hd):
                sl = slice(hd * LANES, (hd + 1) * LANES)
                s = _dot(k_ref[keys, sl], qt_ref[sl, :])
                return jnp.where(mask, s, NEG) if masked else s

            _flash_step(MLA_HEADS, score,
                        lambda hd: vt_ref[hd * MLA_VROWS:(hd + 1) * MLA_VROWS, keys], m_ref, acc_ref, online)

        _for_unmasked_tiles(qi, tile)
        tile(qi, 1, masked=True)

    _softmax_modes(safe_ref[layer], m_ref, acc_ref, run)
    outs = []
    for hd in range(MLA_HEADS):
        a = acc_ref[hd]
        outs.append(a[:MLA_V] / a[MLA_V:MLA_V + 1])
    o_ref[...] = _bf16(jnp.concatenate(outs, axis=0).T)


def _mla_attn(layer, safe, qt, k, vt, batch, seq):
    ta = ATTN_TILE
    nb = seq // ta
    return pl.pallas_call(
        functools.partial(_mla_attn_kernel, layer=layer),
        grid=(batch, nb),
        in_specs=[pl.BlockSpec(memory_space=pltpu.SMEM),
                  pl.BlockSpec((MLA_HEADS * LANES, ta), lambda b, qi: (0, b * nb + qi)),
                  pl.BlockSpec((seq, MLA_HEADS * LANES), lambda b, qi: (b, 0)),
                  pl.BlockSpec((MLA_HEADS * MLA_VROWS, seq), lambda b, qi: (0, b))],
        out_specs=pl.BlockSpec((ta, BRANCH_W), lambda b, qi: (b * nb + qi, 0)),
        out_shape=jax.ShapeDtypeStruct((batch * seq, BRANCH_W), jnp.bfloat16),
        scratch_shapes=[pltpu.VMEM((MLA_HEADS, 1, ta), jnp.float32),
                        pltpu.VMEM((MLA_HEADS, MLA_VROWS, ta), jnp.float32)],
        compiler_params=_params("arbitrary", "arbitrary"),
        name="mla_attn",
    )(safe, qt, k, vt)


def _diff_attn_kernel(safe_ref, ref_ref, lam_ref, q0t_ref, q1t_ref, k_ref, vt_ref,
                      bias_ref, gout_ref, o_ref, m_ref, acc_ref, *, layer, out_scale):
    qi = pl.program_id(1)
    ta = q0t_ref.shape[1]

    def run(online):
        def tile(start, n_tiles, bias_tile=None, masked=False):
            keys = _key_tile(start, n_tiles, ta)
            mask = _chunk_mask_t((ta, ta)) if masked else None

            def score(idx):
                hd, qt_ref = idx // 2, (q0t_ref, q1t_ref)[idx % 2]
                sl = slice(hd * LANES, (hd + 1) * LANES)
                s = _dot(k_ref[keys, sl], qt_ref[sl, :])
                if bias_tile is not None:
                    s = s + bias_ref[bias_tile, hd]
                if not online:
                    s = s - ref_ref[layer * DIFF_HEADS + hd]
                return jnp.where(mask, s, NEG) if masked else s

            _flash_step(2 * DIFF_HEADS, score,
                        lambda idx: vt_ref[(idx // 2) * DIFF_VROWS:(idx // 2 + 1) * DIFF_VROWS, keys],
                        m_ref, acc_ref, online)

        _for_unmasked_tiles(qi - 1, tile)

        @pl.when(qi > 0)
        def _():
            tile(qi - 1, 1, bias_tile=1)

        tile(qi, 1, bias_tile=0, masked=True)

    _softmax_modes(safe_ref[layer], m_ref, acc_ref, run)
    lam = lam_ref[layer]
    g = gout_ref[...] * out_scale
    outs = []
    for hd in range(DIFF_HEADS):
        a0, a1 = acc_ref[2 * hd], acc_ref[2 * hd + 1]
        o = a0[:DIFF_V] / a0[DIFF_V:DIFF_V + 1] - lam * (a1[:DIFF_V] / a1[DIFF_V:DIFF_V + 1])
        outs.append(o * _col_scale(_sumsq(o), DIFF_V) * g)
    o_ref[...] = _bf16(jnp.concatenate(outs, axis=0).T)


def _diff_attn(layer, safe, ref, lam, q0t, q1t, k, vt, bias, g_out, out_scale, batch, seq):
    ta = ATTN_TILE
    nb = seq // ta
    qt_spec = pl.BlockSpec((BRANCH_W, ta), lambda b, qi: (0, b * nb + qi))
    smem = pl.BlockSpec(memory_space=pltpu.SMEM)
    return pl.pallas_call(
        functools.partial(_diff_attn_kernel, layer=layer, out_scale=out_scale),
        grid=(batch, nb),
        in_specs=[smem, smem, smem, qt_spec, qt_spec,
                  pl.BlockSpec((seq, BRANCH_W), lambda b, qi: (b, 0)),
                  pl.BlockSpec((DIFF_HEADS * DIFF_VROWS, seq), lambda b, qi: (0, b)),
                  _single_spec(bias), _layer_spec(g_out, layer, single=True)],
        out_specs=pl.BlockSpec((ta, BRANCH_W), lambda b, qi: (b * nb + qi, 0)),
        out_shape=jax.ShapeDtypeStruct((batch * seq, BRANCH_W), jnp.bfloat16),
        scratch_shapes=[pltpu.VMEM((2 * DIFF_HEADS, 1, ta), jnp.float32),
                        pltpu.VMEM((2 * DIFF_HEADS, DIFF_VROWS, ta), jnp.float32)],
        compiler_params=_params("arbitrary", "arbitrary"),
        name="diff_attn",
    )(safe, ref, lam, q0t, q1t, k, vt, bias, g_out)


def _merge_mlp_kernel(x_ref, oa_ref, ob_ref, oc_ref, gmix_ref, wg_ref, wb_ref, wout_ref,
                      gmlp_ref, w1_ref, w2_ref, out_ref):
    x = x_ref[...]
    h = _bf16(x * _rms_scale(x, D_MODEL) * gmix_ref[...])
    y = None
    for n, o_ref in enumerate((oa_ref, ob_ref, oc_ref)):
        gate = 1.0 / (1.0 + jnp.exp(-_dot(h, wg_ref[:, n * D_MODEL:(n + 1) * D_MODEL])))
        term = gate * _dot(o_ref[...], wb_ref[n])
        y = term if y is None else y + term
    x = x + _dot(_bf16(y), wout_ref[...])
    h2 = _bf16(x * _rms_scale(x, D_MODEL) * gmlp_ref[...])
    for c in range(D_FF // FF_CHUNK):
        f = jnp.maximum(_dot(h2, w1_ref[:, c * FF_CHUNK:(c + 1) * FF_CHUNK]), 0.0)
        x = x + _dot(_bf16(f * f), w2_ref[c * FF_CHUNK:(c + 1) * FF_CHUNK, :])
    out_ref[...] = x


def _merge_mlp(layer, x2, o_a, o_b, o_c, w):
    t = x2.shape[0]
    tm = TOKEN_TILE
    tok = lambda c: pl.BlockSpec((tm, c), lambda i: (i, 0))
    consts = [w["g_mix"], w["w_gate"], w["w_branch"], w["w_out"], w["g_mlp"], w["w_ff1"], w["w_ff2"]]
    return pl.pallas_call(
        _merge_mlp_kernel,
        grid=(t // tm,),
        in_specs=([tok(D_MODEL)] + [tok(BRANCH_W)] * 3
                  + [_layer_spec(c, layer, single=True) for c in consts]),
        out_specs=tok(D_MODEL),
        out_shape=jax.ShapeDtypeStruct((t, D_MODEL), jnp.float32),
        compiler_params=_params("arbitrary"),
        name="merge_mlp",
    )(x2, o_a, o_b, o_c, *consts)


def _transposed_value_weight(w, rows):
    depth, n_in, heads, width = w.shape
    wt = jnp.pad(jnp.transpose(w, (0, 2, 3, 1)), ((0, 0), (0, 0), (0, rows - width), (0, 0)))
    return wt.reshape(depth, heads * rows, n_in)


def _ones_rows(heads, rows, at):
    col = np.zeros((heads, rows, 1), np.float32)
    col[:, at] = 1.0
    return jnp.asarray(col.reshape(heads * rows, 1))


def _layer_weights(w_in, g_mix, g_cq, w_uq, g_ckv, w_ukv, g_mla_q, g_mla_k, g_diff_q, g_diff_k,
                   g_mem_q, w_branch, w_out, g_mlp, w_ff1, w_ff2):
    splits = np.cumsum([Q_LORA, KV_LORA, MLA_ROPE, DIFF_HEADS * 2 * DIFF_QK, DIFF_HEADS * 2 * DIFF_QK,
                        DIFF_HEADS * DIFF_V, MEM_HEADS * MEM_HEAD_DIM]).tolist()
    c_dv, c_mq, c_gate = splits[4], splits[5], splits[6]
    t_in = jnp.swapaxes(w_in[:, :, :c_gate], 1, 2)
    w_dvt = _transposed_value_weight(
        w_in[:, :, c_dv:c_mq].reshape(DEPTH, D_MODEL, DIFF_HEADS, DIFF_V), DIFF_VROWS)
    w_t = jnp.concatenate([t_in[:, :c_dv], w_dvt, t_in[:, c_mq:]], axis=1)
    assert w_t.shape[1] == _R_END
    w_uqt = _transposed_value_weight(w_uq.reshape(DEPTH, Q_LORA, MLA_HEADS, MLA_QK), MLA_QK)
    w_ukv = w_ukv.reshape(DEPTH, KV_LORA, MLA_HEADS, MLA_NOPE + MLA_V)
    w_ukt = _transposed_value_weight(w_ukv[..., :MLA_NOPE], MLA_NOPE)
    w_uvt = _transposed_value_weight(w_ukv[..., MLA_NOPE:], MLA_VROWS)
    row = lambda g: g[:, None, :]
    col = lambda g: g[:, :, None]
    return {
        "g_mix": row(g_mix), "w_t": _bf16(w_t), "g_cq": col(g_cq), "w_uqt": _bf16(w_uqt),
        "g_ckv": col(g_ckv), "w_ukt": _bf16(w_ukt), "w_uvt": _bf16(w_uvt),
        "g_q": col(g_mla_q * (MLA_QK ** -0.5 * LOG2E)), "g_k": col(g_mla_k),
        "g_dq": col(g_diff_q * (DIFF_QK ** -0.5 * LOG2E)), "g_dk": col(g_diff_k),
        "g_mq": col(g_mem_q * MEM_HEAD_DIM ** -0.5),
        "w_gate": _bf16(w_in[:, :, c_gate:]), "w_branch": _bf16(w_branch), "w_out": _bf16(w_out),
        "g_mlp": row(g_mlp), "w_ff1": _bf16(w_ff1), "w_ff2": _bf16(w_ff2),
    }


def _logit_bound(g_q, g_k, n, scale):
    bound = (n * scale * LOG2E * BOUND_MARGIN) * jnp.max(jnp.abs(g_q), axis=-1) * jnp.max(jnp.abs(g_k), axis=-1)
    return _bf16(bound).astype(jnp.float32)


def _softmax_bounds(g_mla_q, g_mla_k, g_diff_q, g_diff_k, bias):
    r_mla = _logit_bound(g_mla_q, g_mla_k, MLA_QK, MLA_QK ** -0.5)
    bias_hi = jnp.max(bias, axis=(0, 2, 3))
    bias_range = jnp.max(bias_hi - jnp.min(bias, axis=(0, 2, 3)))
    r_diff = _logit_bound(g_diff_q, g_diff_k, DIFF_QK, DIFF_QK ** -0.5)
    safe = lambda spread: (spread <= SAFE_LOGIT_SPREAD).astype(jnp.int32)
    return (r_mla, safe(2.0 * r_mla),
            (r_diff[:, None] + bias_hi[None, :]).reshape(-1), safe(2.0 * r_diff + bias_range))


def kernel(x, mem, positions, t5_table, g_mix, g_mem, w_in, g_cq, w_uq, g_ckv, w_ukv, g_mla_q, g_mla_k, g_diff_q, g_diff_k, lam_q1, lam_k1, lam_q2, lam_k2, g_diff_out, w_mem_kv, g_mem_q, g_mem_k, w_branch, w_out, g_mlp, w_ff1, w_ff2):
    b, s, _ = x.shape
    assert s % ATTN_TILE == 0 and s % TOKEN_TILE == 0 and (b * s) % TABLE_TILE == 0
    tabs = _rope_tables(positions)
    bias = _t5_bias_tiles(t5_table)
    km, vmt = _mem_kv(mem, g_mem, w_mem_kv, g_mem_k)
    weights = _layer_weights(w_in, g_mix, g_cq, w_uq, g_ckv, w_ukv, g_mla_q, g_mla_k, g_diff_q,
                             g_diff_k, g_mem_q, w_branch, w_out, g_mlp, w_ff1, w_ff2)
    f32 = jnp.float32
    lam_init = [0.8 - 0.6 * math.exp(-0.3 * l) for l in range(DEPTH)]
    lam = (jnp.exp(jnp.sum(lam_q1.astype(f32) * lam_k1.astype(f32), axis=-1))
           - jnp.exp(jnp.sum(lam_q2.astype(f32) * lam_k2.astype(f32), axis=-1))
           + jnp.asarray(lam_init, f32))
    x2 = x.reshape(b * s, D_MODEL)
    g_out = g_diff_out[:, :, None]
    r_mla, safe_mla, r_diff, safe_diff = _softmax_bounds(g_mla_q, g_mla_k, g_diff_q, g_diff_k, bias)
    weights["q_pad"] = jnp.pad(-r_mla[:, None, None], ((0, 0), (0, LANES - MLA_QK - 1), (0, 0)))
    for l in range(DEPTH):
        qt, k, vt, dq0t, dq1t, dk, dvt, o_c = _proj(l, x2, weights, tabs, km, vmt, s)
        o_a = _mla_attn(l, safe_mla, qt, k, vt, b, s)
        o_b = _diff_attn(l, safe_diff, r_diff, lam, dq0t, dq1t, dk, dvt, bias, g_out,
                         1.0 - lam_init[l], b, s)
        x2 = _merge_mlp(l, x2, o_a, o_b, o_c, weights)
    return x2.reshape(b, s, D_MODEL)
```

```python
import functools
import math

import jax
import jax.numpy as jnp
import numpy as np
from jax import lax
from jax.experimental import pallas as pl
from jax.experimental.pallas import tpu as pltpu

D_MODEL = 1024
DEPTH = 4
CHUNK = 64
MLA_HEADS = 8
MLA_NOPE = 64
MLA_ROPE = 32
MLA_V = 64
MLA_QK = MLA_NOPE + MLA_ROPE
Q_LORA = 384
KV_LORA = 256
ROPE_BASE = 10000.0
DIFF_HEADS = 4
DIFF_QK = 64
DIFF_V = 2 * DIFF_QK
MEM_HEADS = 4
MEM_HEAD_DIM = 128
BRANCH_W = 512
N_BRANCH = 3
D_FF = 4 * D_MODEL
T5_BUCKETS = 32
T5_MAX_DIST = 128
EPS = 1e-6
NEG = -1e30

LANES = 128
BF16_SUBLANES = 16
V7X_VMEM_BYTES = 64 * 1024 * 1024
VMEM_LIMIT = V7X_VMEM_BYTES * 7 // 8

TOKEN_TILE = 512
ATTN_TILE = 512
FAR_TILES = 2
TABLE_TILE = 4096
FF_CHUNK = 1024
ROPE_HALF = MLA_ROPE // 2


def _round_up(n, m):
    return -(-n // m) * m


MLA_VROWS = _round_up(MLA_V + 1, BF16_SUBLANES)
DIFF_VROWS = _round_up(DIFF_V + 1, BF16_SUBLANES)
MEM_VROWS = _round_up(MEM_HEAD_DIM + 1, BF16_SUBLANES)
LOG2E = math.log2(math.e)

SAFE_LOGIT_SPREAD = 120.0
BOUND_MARGIN = 1.02

_R_CQ = 0
_R_CKV = _R_CQ + Q_LORA
_R_KR = _R_CKV + KV_LORA
_R_DQ = _R_KR + MLA_ROPE
_R_DK = _R_DQ + DIFF_HEADS * LANES
_R_DV = _R_DK + DIFF_HEADS * LANES
_R_MQ = _R_DV + DIFF_HEADS * DIFF_VROWS
_R_END = _R_MQ + MEM_HEADS * MEM_HEAD_DIM


def _bf16(x):
    return x.astype(jnp.bfloat16)


def _dot(a, b):
    return jnp.dot(a, b, preferred_element_type=jnp.float32)


def _dot_nt(a, b):
    return lax.dot_general(a, b, (((1,), (1,)), ((), ())), preferred_element_type=jnp.float32)


def _rms_scale(x, n):
    return lax.rsqrt(jnp.sum(x * x, axis=-1, keepdims=True) * (1.0 / n) + EPS)


def _sumsq(x):
    return jnp.sum(x * x, axis=0, keepdims=True)


def _col_scale(sumsq, n):
    return lax.rsqrt(sumsq * (1.0 / n) + EPS)


def _const_spec(shape):
    return pl.BlockSpec(shape, lambda *_: (0,) * len(shape))


def _single_spec(a):
    return pl.BlockSpec(a.shape, lambda *_: (0,) * a.ndim, pipeline_mode=pl.Buffered(1))


def _layer_spec(a, layer, single=False):
    return pl.BlockSpec((None,) + a.shape[1:], lambda *_: (layer,) + (0,) * (a.ndim - 1),
                        pipeline_mode=pl.Buffered(1) if single else None)


def _params(*sem):
    return pltpu.CompilerParams(dimension_semantics=sem, vmem_limit_bytes=VMEM_LIMIT)


def _rope_tables_kernel(pos_ref, inv_ref, cos_ref, sin_ref):
    ang = inv_ref[...] * pos_ref[...].astype(jnp.float32)
    cos_ref[...] = jnp.cos(ang)
    sin_ref[...] = jnp.sin(ang)


def _rope_tables(positions):
    t = positions.size
    inv = jnp.power(jnp.float32(ROPE_BASE), -jnp.arange(ROPE_HALF, dtype=jnp.float32) / ROPE_HALF)
    out = jax.ShapeDtypeStruct((ROPE_HALF, t), jnp.float32)
    return pl.pallas_call(
        _rope_tables_kernel,
        grid=(t // TABLE_TILE,),
        in_specs=[pl.BlockSpec((1, TABLE_TILE), lambda i: (0, i)), _const_spec((ROPE_HALF, 1))],
        out_specs=[pl.BlockSpec((ROPE_HALF, TABLE_TILE), lambda i: (0, i))] * 2,
        out_shape=[out] * 2,
        compiler_params=_params("arbitrary"),
        name="rope_tables",
    )(positions.reshape(1, t), inv.reshape(ROPE_HALF, 1))


def _t5_large_thresholds():
    n = T5_BUCKETS // 2
    max_exact = n // 2
    assert T5_MAX_DIST == 16 * max_exact and n - max_exact == 8
    out = []
    for j in range(1, n - max_exact):
        a = max_exact
        while a * a < max_exact * max_exact * 2 ** j:
            a += 1
        out.append(a)
    return out


def _t5_bias_kernel(table_ref, bias_ref):
    t = bias_ref.shape[-1]
    off = pl.program_id(0) * t
    row = lax.broadcasted_iota(jnp.int32, (t, t), 0)
    col = lax.broadcasted_iota(jnp.int32, (t, t), 1)
    rel = row - col - off
    n = T5_BUCKETS // 2
    max_exact = n // 2
    a = jnp.abs(rel)
    large = jnp.full((t, t), max_exact, jnp.int32)
    for thr in _t5_large_thresholds():
        large = large + (a >= thr).astype(jnp.int32)
    bucket = jnp.where(rel > 0, n, 0) + jnp.where(a < max_exact, a, large)
    far_bucket = n - 1
    for h in range(DIFF_HEADS):
        val = jnp.full((t, t), table_ref[0, h], jnp.float32)
        for b in range(1, T5_BUCKETS):
            val = jnp.where(bucket == b, table_ref[b, h], val)
        bias_ref[0, h] = (val - table_ref[far_bucket, h]) * LOG2E


def _t5_bias_tiles(t5_table):
    t = ATTN_TILE
    assert t >= _t5_large_thresholds()[-1]
    return pl.pallas_call(
        _t5_bias_kernel,
        grid=(2,),
        in_specs=[pl.BlockSpec(memory_space=pltpu.SMEM)],
        out_specs=pl.BlockSpec((1, DIFF_HEADS, t, t), lambda i: (i, 0, 0, 0)),
        out_shape=jax.ShapeDtypeStruct((2, DIFF_HEADS, t, t), jnp.float32),
        compiler_params=_params("arbitrary"),
        name="t5_bias_tiles",
    )(t5_table.astype(jnp.float32))


def _mem_kv_kernel(mem_ref, g_ref, wk_ref, wvt_ref, ones_ref, gk_ref, km_ref, vmt_ref):
    x = mem_ref[0]
    h = _bf16(x * _rms_scale(x, D_MODEL) * g_ref[0])
    k = _dot(h, wk_ref[0])
    for hd in range(MEM_HEADS):
        sl = slice(hd * MEM_HEAD_DIM, (hd + 1) * MEM_HEAD_DIM)
        kh = k[:, sl]
        km_ref[0, 0, :, sl] = _bf16(kh * _rms_scale(kh, MEM_HEAD_DIM) * gk_ref[0])
    vmt_ref[0, 0] = _bf16(_dot_nt(wvt_ref[0], h) + ones_ref[...])


def _mem_kv(mem, g_mem, w_mem_kv, g_mem_k):
    b, m, _ = mem.shape
    width = MEM_HEADS * MEM_HEAD_DIM
    rows = MEM_HEADS * MEM_VROWS
    w_k = _bf16(w_mem_kv[:, :, :width])
    w_vt = _bf16(_transposed_value_weight(
        w_mem_kv[:, :, width:].reshape(DEPTH, D_MODEL, MEM_HEADS, MEM_HEAD_DIM), MEM_VROWS))
    ones = _ones_rows(MEM_HEADS, MEM_VROWS, MEM_HEAD_DIM)
    layer = lambda l, i: (l, 0, 0)
    return pl.pallas_call(
        _mem_kv_kernel,
        grid=(DEPTH, b),
        in_specs=[
            pl.BlockSpec((1, m, D_MODEL), lambda l, i: (i, 0, 0)),
            pl.BlockSpec((1, 1, D_MODEL), layer),
            pl.BlockSpec((1, D_MODEL, width), layer),
            pl.BlockSpec((1, rows, D_MODEL), layer),
            _const_spec(ones.shape),
            pl.BlockSpec((1, 1, MEM_HEAD_DIM), layer),
        ],
        out_specs=[pl.BlockSpec((1, 1, m, width), lambda l, i: (l, i, 0, 0)),
                   pl.BlockSpec((1, 1, rows, m), lambda l, i: (l, i, 0, 0))],
        out_shape=[jax.ShapeDtypeStruct((DEPTH, b, m, width), jnp.bfloat16),
                   jax.ShapeDtypeStruct((DEPTH, b, rows, m), jnp.bfloat16)],
        compiler_params=_params("arbitrary", "arbitrary"),
        name="mem_kv",
    )(mem, g_mem.reshape(DEPTH, 1, D_MODEL), w_k, w_vt, ones, g_mem_k.reshape(DEPTH, 1, MEM_HEAD_DIM))


def _rope_rows(x1, x2, cos_t, sin_t):
    return x1 * cos_t - x2 * sin_t, x2 * cos_t + x1 * sin_t


def _proj_kernel(x_ref, gmix_ref, wt_ref, gcq_ref, wuqt_ref, gckv_ref, wukt_ref, wuvt_ref, vones_ref,
                 dvones_ref, gq_ref, gk_ref, qpad_ref, kpad_ref, cos_ref, sin_ref, gdq_ref, gdk_ref, gmq_ref,
                 km_ref, vmt_ref,
                 qt_ref, k_ref, vt_ref, dq0t_ref, dq1t_ref, dk_ref, dvt_ref, oc_ref):
    x = x_ref[...]
    tm = x.shape[0]
    h = _bf16(x * _rms_scale(x, D_MODEL) * gmix_ref[...])
    zt = _dot_nt(wt_ref[...], h)
    cos_t, sin_t = cos_ref[...], sin_ref[...]

    c_q = zt[_R_CQ:_R_CKV]
    cq_n = _bf16(c_q * _col_scale(_sumsq(c_q), Q_LORA) * gcq_ref[...])
    qt_raw = _dot(wuqt_ref[...], cq_n)
    c_kv = zt[_R_CKV:_R_KR]
    ckv_n = _bf16(c_kv * _col_scale(_sumsq(c_kv), KV_LORA) * gckv_ref[...])
    k_nope = _dot(wukt_ref[...], ckv_n)
    vt_ref[...] = _bf16(_dot(wuvt_ref[...], ckv_n) + vones_ref[...])
    gq, gk = gq_ref[...], gk_ref[...]
    k_r = zt[_R_KR:_R_DQ]
    kr_ss = _sumsq(k_r)
    kr_g = k_r * gk[MLA_NOPE:]
    kr1, kr2 = _rope_rows(kr_g[:ROPE_HALF], kr_g[ROPE_HALF:], cos_t, sin_t)
    q_pad = jnp.broadcast_to(qpad_ref[...], (LANES - MLA_QK, tm))
    k_pad = jnp.broadcast_to(kpad_ref[...], (LANES - MLA_QK, tm))
    for hd in range(MLA_HEADS):
        rows = slice(hd * LANES, (hd + 1) * LANES)
        q = qt_raw[hd * MLA_QK:(hd + 1) * MLA_QK]
        qn = q * _col_scale(_sumsq(q), MLA_QK) * gq
        q1, q2 = _rope_rows(qn[MLA_NOPE:MLA_NOPE + ROPE_HALF], qn[MLA_NOPE + ROPE_HALF:], cos_t, sin_t)
        qt_ref[rows, :] = _bf16(jnp.concatenate([qn[:MLA_NOPE], q1, q2, q_pad], axis=0))
        kn = k_nope[hd * MLA_NOPE:(hd + 1) * MLA_NOPE]
        rs = _col_scale(_sumsq(kn) + kr_ss, MLA_QK)
        kt = jnp.concatenate([kn * rs * gk[:MLA_NOPE], kr1 * rs, kr2 * rs, k_pad], axis=0)
        k_ref[:, rows] = _bf16(kt.T)

    gdq, gdk = gdq_ref[...], gdk_ref[...]
    zero_half = jnp.zeros((DIFF_QK, tm), jnp.float32)

    def half_norm(t, g):
        return t * _col_scale(_sumsq(t), DIFF_QK) * g

    for hd in range(DIFF_HEADS):
        rows = slice(hd * LANES, (hd + 1) * LANES)
        r0 = _R_DQ + hd * LANES
        c0 = half_norm(zt[r0:r0 + DIFF_QK], gdq)
        c1 = half_norm(zt[r0 + DIFF_QK:r0 + LANES], gdq)
        dq0t_ref[rows, :] = _bf16(jnp.concatenate([c0, zero_half], axis=0))
        dq1t_ref[rows, :] = _bf16(jnp.concatenate([zero_half, c1], axis=0))
        r0 = _R_DK + hd * LANES
        k0, k1 = half_norm(zt[r0:r0 + DIFF_QK], gdk), half_norm(zt[r0 + DIFF_QK:r0 + LANES], gdk)
        dk_ref[:, rows] = _bf16(jnp.concatenate([k0, k1], axis=0).T)
    dvt_ref[...] = _bf16(zt[_R_DV:_R_MQ] + dvones_ref[...])

    gmq = gmq_ref[...]
    for hd in range(MEM_HEADS):
        sl = slice(hd * MEM_HEAD_DIM, (hd + 1) * MEM_HEAD_DIM)
        mq = zt[_R_MQ + hd * MEM_HEAD_DIM:_R_MQ + (hd + 1) * MEM_HEAD_DIM]
        mq = _bf16(mq * _col_scale(_sumsq(mq), MEM_HEAD_DIM) * gmq)
        s = _dot(km_ref[0, 0, :, sl], mq)
        p = _bf16(jnp.exp(s - jnp.max(s, axis=0, keepdims=True)))
        o = _dot(vmt_ref[0, 0, hd * MEM_VROWS:(hd + 1) * MEM_VROWS, :], p)
        o = o[:MEM_HEAD_DIM] / o[MEM_HEAD_DIM:MEM_HEAD_DIM + 1]
        oc_ref[:, sl] = _bf16(o.T)


def _proj(layer, x2, w, tabs, km, vmt, seq):
    t = x2.shape[0]
    tm = TOKEN_TILE
    steps_per_batch = seq // tm
    tok = lambda c: pl.BlockSpec((tm, c), lambda i: (i, 0))
    tokt = lambda r: pl.BlockSpec((r, tm), lambda i: (0, i))
    mem_map = lambda i: (layer, i // steps_per_batch, 0, 0)
    bf = jnp.bfloat16
    stacked = lambda name: (w[name], _layer_spec(w[name], layer))
    fixed = lambda a: (a, _const_spec(a.shape))
    consts = [stacked("g_mix"), stacked("w_t"), stacked("g_cq"), stacked("w_uqt"), stacked("g_ckv"),
              stacked("w_ukt"), stacked("w_uvt"), fixed(_ones_rows(MLA_HEADS, MLA_VROWS, MLA_V)),
              fixed(_ones_rows(DIFF_HEADS, DIFF_VROWS, DIFF_V)), stacked("g_q"), stacked("g_k"),
              stacked("q_pad"), fixed(_ones_rows(1, LANES - MLA_QK, 0))]
    gains = [stacked("g_dq"), stacked("g_dk"), stacked("g_mq")]
    wide, vt_rows, dvt_rows = MLA_HEADS * LANES, MLA_HEADS * MLA_VROWS, DIFF_HEADS * DIFF_VROWS
    outs = [(tokt(wide), (wide, t)), (tok(wide), (t, wide)), (tokt(vt_rows), (vt_rows, t)),
            (tokt(BRANCH_W), (BRANCH_W, t)), (tokt(BRANCH_W), (BRANCH_W, t)), (tok(BRANCH_W), (t, BRANCH_W)),
            (tokt(dvt_rows), (dvt_rows, t)), (tok(BRANCH_W), (t, BRANCH_W))]
    return pl.pallas_call(
        _proj_kernel,
        grid=(t // tm,),
        in_specs=([tok(D_MODEL)] + [spec for _, spec in consts] + [tokt(ROPE_HALF)] * 2
                  + [spec for _, spec in gains]
                  + [pl.BlockSpec((1, 1) + km.shape[2:], mem_map), pl.BlockSpec((1, 1) + vmt.shape[2:], mem_map)]),
        out_specs=[spec for spec, _ in outs],
        out_shape=[jax.ShapeDtypeStruct(shape, bf) for _, shape in outs],
        compiler_params=_params("arbitrary"),
        name="proj",
    )(x2, *[a for a, _ in consts], *tabs, *[a for a, _ in gains], km, vmt)


def _chunk_mask_t(shape):
    krow = lax.broadcasted_iota(jnp.int32, shape, 0)
    qcol = lax.broadcasted_iota(jnp.int32, shape, 1)
    return (krow // CHUNK) <= (qcol // CHUNK)


def _flash_step(n, score_fn, vt_fn, m_ref, acc_ref, online):
    def scores_and_max(i):
        s = score_fn(i)
        if not online:
            return s, None, None
        m_old = m_ref[i]
        m_new = jnp.maximum(m_old, jnp.max(s, axis=0, keepdims=True))
        m_ref[i] = m_new
        return s, m_new, jnp.exp2(m_old - m_new)

    def exponentials(a):
        s, m_new, alpha = a
        return _bf16(jnp.exp2(s if m_new is None else s - m_new)), alpha

    def value_product(i, b):
        p, alpha = b
        acc = acc_ref[i] if alpha is None else alpha * acc_ref[i]
        acc_ref[i] = acc + _dot(vt_fn(i), p)

    stage_a, stage_b = {}, {}
    for it in range(n + 2):
        if it < n:
            stage_a[it] = scores_and_max(it)
        if 0 <= it - 1 < n:
            stage_b[it - 1] = exponentials(stage_a.pop(it - 1))
        if 0 <= it - 2 < n:
            value_product(it - 2, stage_b.pop(it - 2))


def _softmax_modes(safe, m_ref, acc_ref, run):
    acc_ref[...] = jnp.zeros(acc_ref.shape, jnp.float32)

    @pl.when(safe == 1)
    def _():
        run(False)

    @pl.when(safe != 1)
    def _():
        m_ref[...] = jnp.full(m_ref.shape, NEG, jnp.float32)
        run(True)


def _key_tile(start_tile, n_tiles, ta):
    return pl.ds(pl.multiple_of(start_tile * ta, ta), n_tiles * ta)


def _for_unmasked_tiles(n, tile):
    def body(i, carry):
        tile(i * FAR_TILES, FAR_TILES)
        return carry

    n = jnp.maximum(n, 0)
    lax.fori_loop(0, n // FAR_TILES, body, 0)
    for r in range(1, FAR_TILES):
        @pl.when(n % FAR_TILES >= r)
        def _():
            tile(n - n % FAR_TILES + (r - 1), 1)


def _mla_attn_kernel(safe_ref, qt_ref, k_ref, vt_ref, o_ref, m_ref, acc_ref, *, layer):
    qi = pl.program_id(1)
    ta = qt_ref.shape[1]

    def run(online):
        def tile(start, n_tiles, masked=False):
            keys = _key_tile(start, n_tiles, ta)
            mask = _chunk_mask_t((ta, ta)) if masked else None

            def score(hd):
                sl = slice(hd * LANES, (hd + 1) * LANES)
                s = _dot(k_ref[keys, sl], qt_ref[sl, :])
                return jnp.where(mask, s, NEG) if masked else s

            _flash_step(MLA_HEADS, score,
                        lambda hd: vt_ref[hd * MLA_VROWS:(hd + 1) * MLA_VROWS, keys], m_ref, acc_ref, online)

        _for_unmasked_tiles(qi, tile)
        tile(qi, 1, masked=True)

    _softmax_modes(safe_ref[layer], m_ref, acc_ref, run)
    outs = []
    for hd in range(MLA_HEADS):
        a = acc_ref[hd]
        outs.append(a[:MLA_V] / a[MLA_V:MLA_V + 1])
    o_ref[...] = _bf16(jnp.concatenate(outs, axis=0).T)


def _mla_attn(layer, safe, qt, k, vt, batch, seq):
    ta = ATTN_TILE
    nb = seq // ta
    return pl.pallas_call(
        functools.partial(_mla_attn_kernel, layer=layer),
        grid=(batch, nb),
        in_specs=[pl.BlockSpec(memory_space=pltpu.SMEM),
                  pl.BlockSpec((MLA_HEADS * LANES, ta), lambda b, qi: (0, b * nb + qi)),
                  pl.BlockSpec((seq, MLA_HEADS * LANES), lambda b, qi: (b, 0)),
                  pl.BlockSpec((MLA_HEADS * MLA_VROWS, seq), lambda b, qi: (0, b))],
        out_specs=pl.BlockSpec((ta, BRANCH_W), lambda b, qi: (b * nb + qi, 0)),
        out_shape=jax.ShapeDtypeStruct((batch * seq, BRANCH_W), jnp.bfloat16),
        scratch_shapes=[pltpu.VMEM((MLA_HEADS, 1, ta), jnp.float32),
                        pltpu.VMEM((MLA_HEADS, MLA_VROWS, ta), jnp.float32)],
        compiler_params=_params("arbitrary", "arbitrary"),
        name="mla_attn",
    )(safe, qt, k, vt)


def _diff_attn_kernel(safe_ref, ref_ref, lam_ref, q0t_ref, q1t_ref, k_ref, vt_ref,
                      bias_ref, gout_ref, o_ref, m_ref, acc_ref, *, layer, out_scale):
    qi = pl.program_id(1)
    ta = q0t_ref.shape[1]

    def run(online):
        def tile(start, n_tiles, bias_tile=None, masked=False):
            keys = _key_tile(start, n_tiles, ta)
            mask = _chunk_mask_t((ta, ta)) if masked else None

            def score(idx):
                hd, qt_ref = idx // 2, (q0t_ref, q1t_ref)[idx % 2]
                sl = slice(hd * LANES, (hd + 1) * LANES)
                s = _dot(k_ref[keys, sl], qt_ref[sl, :])
                if bias_tile is not None:
                    s = s + bias_ref[bias_tile, hd]
                if not online:
                    s = s - ref_ref[layer * DIFF_HEADS + hd]
                return jnp.where(mask, s, NEG) if masked else s

            _flash_step(2 * DIFF_HEADS, score,
                        lambda idx: vt_ref[(idx // 2) * DIFF_VROWS:(idx // 2 + 1) * DIFF_VROWS, keys],
                        m_ref, acc_ref, online)

        _for_unmasked_tiles(qi - 1, tile)

        @pl.when(qi > 0)
        def _():
            tile(qi - 1, 1, bias_tile=1)

        tile(qi, 1, bias_tile=0, masked=True)

    _softmax_modes(safe_ref[layer], m_ref, acc_ref, run)
    lam = lam_ref[layer]
    g = gout_ref[...] * out_scale
    outs = []
    for hd in range(DIFF_HEADS):
        a0, a1 = acc_ref[2 * hd], acc_ref[2 * hd + 1]
        o = a0[:DIFF_V] / a0[DIFF_V:DIFF_V + 1] - lam * (a1[:DIFF_V] / a1[DIFF_V:DIFF_V + 1])
        outs.append(o * _col_scale(_sumsq(o), DIFF_V) * g)
    o_ref[...] = _bf16(jnp.concatenate(outs, axis=0).T)


def _diff_attn(layer, safe, ref, lam, q0t, q1t, k, vt, bias, g_out, out_scale, batch, seq):
    ta = ATTN_TILE
    nb = seq // ta
    qt_spec = pl.BlockSpec((BRANCH_W, ta), lambda b, qi: (0, b * nb + qi))
    smem = pl.BlockSpec(memory_space=pltpu.SMEM)
    return pl.pallas_call(
        functools.partial(_diff_attn_kernel, layer=layer, out_scale=out_scale),
        grid=(batch, nb),
        in_specs=[smem, smem, smem, qt_spec, qt_spec,
                  pl.BlockSpec((seq, BRANCH_W), lambda b, qi: (b, 0)),
                  pl.BlockSpec((DIFF_HEADS * DIFF_VROWS, seq), lambda b, qi: (0, b)),
                  _single_spec(bias), _layer_spec(g_out, layer, single=True)],
        out_specs=pl.BlockSpec((ta, BRANCH_W), lambda b, qi: (b * nb + qi, 0)),
        out_shape=jax.ShapeDtypeStruct((batch * seq, BRANCH_W), jnp.bfloat16),
        scratch_shapes=[pltpu.VMEM((2 * DIFF_HEADS, 1, ta), jnp.float32),
                        pltpu.VMEM((2 * DIFF_HEADS, DIFF_VROWS, ta), jnp.float32)],
        compiler_params=_params("arbitrary", "arbitrary"),
        name="diff_attn",
    )(safe, ref, lam, q0t, q1t, k, vt, bias, g_out)


def _merge_mlp_kernel(x_ref, oa_ref, ob_ref, oc_ref, gmix_ref, wg_ref, wb_ref, wout_ref,
                      gmlp_ref, w1_ref, w2_ref, out_ref):
    branch = [_dot(o_ref[...], wb_ref[n]) for n, o_ref in enumerate((oa_ref, ob_ref, oc_ref))]
    x = x_ref[...]
    h = _bf16(x * _rms_scale(x, D_MODEL) * gmix_ref[...])
    y = None
    for n in range(N_BRANCH):
        gate = 1.0 / (1.0 + jnp.exp(-_dot(h, wg_ref[:, n * D_MODEL:(n + 1) * D_MODEL])))
        term = gate * branch[n]
        y = term if y is None else y + term
    x = x + _dot(_bf16(y), wout_ref[...])
    h2 = _bf16(x * _rms_scale(x, D_MODEL) * gmlp_ref[...])
    for c in range(D_FF // FF_CHUNK):
        f = jnp.maximum(_dot(h2, w1_ref[:, c * FF_CHUNK:(c + 1) * FF_CHUNK]), 0.0)
        x = x + _dot(_bf16(f * f), w2_ref[c * FF_CHUNK:(c + 1) * FF_CHUNK, :])
    out_ref[...] = x


def _merge_mlp(layer, x2, o_a, o_b, o_c, w):
    t = x2.shape[0]
    tm = TOKEN_TILE
    tok = lambda c: pl.BlockSpec((tm, c), lambda i: (i, 0))
    consts = [w["g_mix"], w["w_gate"], w["w_branch"], w["w_out"], w["g_mlp"], w["w_ff1"], w["w_ff2"]]
    return pl.pallas_call(
        _merge_mlp_kernel,
        grid=(t // tm,),
        in_specs=([tok(D_MODEL)] + [tok(BRANCH_W)] * 3
                  + [_layer_spec(c, layer, single=True) for c in consts]),
        out_specs=tok(D_MODEL),
        out_shape=jax.ShapeDtypeStruct((t, D_MODEL), jnp.float32),
        compiler_params=_params("arbitrary"),
        name="merge_mlp",
    )(x2, o_a, o_b, o_c, *consts)


def _transposed_value_weight(w, rows):
    depth, n_in, heads, width = w.shape
    wt = jnp.pad(jnp.transpose(w, (0, 2, 3, 1)), ((0, 0), (0, 0), (0, rows - width), (0, 0)))
    return wt.reshape(depth, heads * rows, n_in)


def _ones_rows(heads, rows, at):
    col = np.zeros((heads, rows, 1), np.float32)
    col[:, at] = 1.0
    return jnp.asarray(col.reshape(heads * rows, 1))


def _layer_weights(w_in, g_mix, g_cq, w_uq, g_ckv, w_ukv, g_mla_q, g_mla_k, g_diff_q, g_diff_k,
                   g_mem_q, w_branch, w_out, g_mlp, w_ff1, w_ff2):
    w_in, w_uq, w_ukv = _bf16(w_in), _bf16(w_uq), _bf16(w_ukv)
    splits = np.cumsum([Q_LORA, KV_LORA, MLA_ROPE, DIFF_HEADS * 2 * DIFF_QK, DIFF_HEADS * 2 * DIFF_QK,
                        DIFF_HEADS * DIFF_V, MEM_HEADS * MEM_HEAD_DIM]).tolist()
    c_dv, c_mq, c_gate = splits[4], splits[5], splits[6]
    t_in = jnp.swapaxes(w_in[:, :, :c_gate], 1, 2)
    w_dvt = _transposed_value_weight(
        w_in[:, :, c_dv:c_mq].reshape(DEPTH, D_MODEL, DIFF_HEADS, DIFF_V), DIFF_VROWS)
    w_t = jnp.concatenate([t_in[:, :c_dv], w_dvt, t_in[:, c_mq:]], axis=1)
    assert w_t.shape[1] == _R_END
    w_uqt = _transposed_value_weight(w_uq.reshape(DEPTH, Q_LORA, MLA_HEADS, MLA_QK), MLA_QK)
    w_ukv = w_ukv.reshape(DEPTH, KV_LORA, MLA_HEADS, MLA_NOPE + MLA_V)
    w_ukt = _transposed_value_weight(w_ukv[..., :MLA_NOPE], MLA_NOPE)
    w_uvt = _transposed_value_weight(w_ukv[..., MLA_NOPE:], MLA_VROWS)
    row = lambda g: g[:, None, :]
    col = lambda g: g[:, :, None]
    return {
        "g_mix": row(g_mix), "w_t": w_t, "g_cq": col(g_cq), "w_uqt": w_uqt,
        "g_ckv": col(g_ckv), "w_ukt": w_ukt, "w_uvt": w_uvt,
        "g_q": col(g_mla_q * (MLA_QK ** -0.5 * LOG2E)), "g_k": col(g_mla_k),
        "g_dq": col(g_diff_q * (DIFF_QK ** -0.5 * LOG2E)), "g_dk": col(g_diff_k),
        "g_mq": col(g_mem_q * MEM_HEAD_DIM ** -0.5),
        "w_gate": w_in[:, :, c_gate:], "w_branch": _bf16(w_branch), "w_out": _bf16(w_out),
        "g_mlp": row(g_mlp), "w_ff1": _bf16(w_ff1), "w_ff2": _bf16(w_ff2),
    }


def _logit_bound(g_q, g_k, n, scale):
    bound = (n * scale * LOG2E * BOUND_MARGIN) * jnp.max(jnp.abs(g_q), axis=-1) * jnp.max(jnp.abs(g_k), axis=-1)
    return _bf16(bound).astype(jnp.float32)


def _softmax_bounds(g_mla_q, g_mla_k, g_diff_q, g_diff_k, bias):
    r_mla = _logit_bound(g_mla_q, g_mla_k, MLA_QK, MLA_QK ** -0.5)
    bias_hi = jnp.max(bias, axis=(0, 2, 3))
    bias_range = jnp.max(bias_hi - jnp.min(bias, axis=(0, 2, 3)))
    r_diff = _logit_bound(g_diff_q, g_diff_k, DIFF_QK, DIFF_QK ** -0.5)
    safe = lambda spread: (spread <= SAFE_LOGIT_SPREAD).astype(jnp.int32)
    return (r_mla, safe(2.0 * r_mla),
            (r_diff[:, None] + bias_hi[None, :]).reshape(-1), safe(2.0 * r_diff + bias_range))


def kernel(x, mem, positions, t5_table, g_mix, g_mem, w_in, g_cq, w_uq, g_ckv, w_ukv, g_mla_q, g_mla_k, g_diff_q, g_diff_k, lam_q1, lam_k1, lam_q2, lam_k2, g_diff_out, w_mem_kv, g_mem_q, g_mem_k, w_branch, w_out, g_mlp, w_ff1, w_ff2):
    b, s, _ = x.shape
    assert s % ATTN_TILE == 0 and s % TOKEN_TILE == 0 and (b * s) % TABLE_TILE == 0
    tabs = _rope_tables(positions)
    bias = _t5_bias_tiles(t5_table)
    km, vmt = _mem_kv(mem, g_mem, w_mem_kv, g_mem_k)
    weights = _layer_weights(w_in, g_mix, g_cq, w_uq, g_ckv, w_ukv, g_mla_q, g_mla_k, g_diff_q,
                             g_diff_k, g_mem_q, w_branch, w_out, g_mlp, w_ff1, w_ff2)
    f32 = jnp.float32
    lam_init = [0.8 - 0.6 * math.exp(-0.3 * l) for l in range(DEPTH)]
    lam = (jnp.exp(jnp.sum(lam_q1.astype(f32) * lam_k1.astype(f32), axis=-1))
           - jnp.exp(jnp.sum(lam_q2.astype(f32) * lam_k2.astype(f32), axis=-1))
           + jnp.asarray(lam_init, f32))
    x2 = x.reshape(b * s, D_MODEL)
    g_out = g_diff_out[:, :, None]
    r_mla, safe_mla, r_diff, safe_diff = _softmax_bounds(g_mla_q, g_mla_k, g_diff_q, g_diff_k, bias)
    weights["q_pad"] = jnp.pad(-r_mla[:, None, None], ((0, 0), (0, LANES - MLA_QK - 1), (0, 0)))
    for l in range(DEPTH):
        qt, k, vt, dq0t, dq1t, dk, dvt, o_c = _proj(l, x2, weights, tabs, km, vmt, s)
        o_a = _mla_attn(l, safe_mla, qt, k, vt, b, s)
        o_b = _diff_attn(l, safe_diff, r_diff, lam, dq0t, dq1t, dk, dvt, bias, g_out,
                         1.0 - lam_init[l], b, s)
        x2 = _merge_mlp(l, x2, o_a, o_b, o_c, weights)
    return x2.reshape(b, s, D_MODEL)
```

```python
import functools
import math

import jax
import jax.numpy as jnp
import numpy as np
from jax import lax
from jax.experimental import pallas as pl
from jax.experimental.pallas import tpu as pltpu

D_MODEL = 1024
DEPTH = 4
CHUNK = 64
MLA_HEADS = 8
MLA_NOPE = 64
MLA_ROPE = 32
MLA_V = 64
MLA_QK = MLA_NOPE + MLA_ROPE
Q_LORA = 384
KV_LORA = 256
ROPE_BASE = 10000.0
DIFF_HEADS = 4
DIFF_QK = 64
DIFF_V = 2 * DIFF_QK
MEM_HEADS = 4
MEM_HEAD_DIM = 128
BRANCH_W = 512
N_BRANCH = 3
D_FF = 4 * D_MODEL
T5_BUCKETS = 32
T5_MAX_DIST = 128
EPS = 1e-6
NEG = -1e30

LANES = 128
BF16_SUBLANES = 16
V7X_VMEM_BYTES = 64 * 1024 * 1024
VMEM_LIMIT = V7X_VMEM_BYTES * 7 // 8

TOKEN_TILE = 512
PROJ_TILE = 1024
ATTN_TILE = 512
FAR_TILES = 2
TABLE_TILE = 4096
FF_CHUNK = 1024
ROPE_HALF = MLA_ROPE // 2


def _round_up(n, m):
    return -(-n // m) * m


MLA_VROWS = _round_up(MLA_V + 1, BF16_SUBLANES)
DIFF_VROWS = _round_up(DIFF_V + 1, BF16_SUBLANES)
MEM_VROWS = _round_up(MEM_HEAD_DIM + 1, BF16_SUBLANES)
LOG2E = math.log2(math.e)

SAFE_LOGIT_SPREAD = 120.0
BOUND_MARGIN = 1.02

_R_CQ = 0
_R_CKV = _R_CQ + Q_LORA
_R_KR = _R_CKV + KV_LORA
_R_DQ = _R_KR + MLA_ROPE
_R_DK = _R_DQ + DIFF_HEADS * LANES
_R_DV = _R_DK + DIFF_HEADS * LANES
_R_MQ = _R_DV + DIFF_HEADS * DIFF_VROWS
_R_END = _R_MQ + MEM_HEADS * MEM_HEAD_DIM


def _bf16(x):
    return x.astype(jnp.bfloat16)


def _dot(a, b):
    return jnp.dot(a, b, preferred_element_type=jnp.float32)


def _dot_nt(a, b):
    return lax.dot_general(a, b, (((1,), (1,)), ((), ())), preferred_element_type=jnp.float32)


def _rms_scale(x, n):
    return lax.rsqrt(jnp.sum(x * x, axis=-1, keepdims=True) * (1.0 / n) + EPS)


def _sumsq(x):
    return jnp.sum(x * x, axis=0, keepdims=True)


def _col_scale(sumsq, n):
    return lax.rsqrt(sumsq * (1.0 / n) + EPS)


def _const_spec(shape):
    return pl.BlockSpec(shape, lambda *_: (0,) * len(shape))


def _single_spec(a):
    return pl.BlockSpec(a.shape, lambda *_: (0,) * a.ndim, pipeline_mode=pl.Buffered(1))


def _layer_spec(a, layer, single=False):
    return pl.BlockSpec((None,) + a.shape[1:], lambda *_: (layer,) + (0,) * (a.ndim - 1),
                        pipeline_mode=pl.Buffered(1) if single else None)


def _params(*sem):
    return pltpu.CompilerParams(dimension_semantics=sem, vmem_limit_bytes=VMEM_LIMIT)


def _rope_tables_kernel(pos_ref, inv_ref, cos_ref, sin_ref):
    ang = inv_ref[...] * pos_ref[...].astype(jnp.float32)
    cos_ref[...] = jnp.cos(ang)
    sin_ref[...] = jnp.sin(ang)


def _rope_tables(positions):
    t = positions.size
    inv = jnp.power(jnp.float32(ROPE_BASE), -jnp.arange(ROPE_HALF, dtype=jnp.float32) / ROPE_HALF)
    out = jax.ShapeDtypeStruct((ROPE_HALF, t), jnp.float32)
    return pl.pallas_call(
        _rope_tables_kernel,
        grid=(t // TABLE_TILE,),
        in_specs=[pl.BlockSpec((1, TABLE_TILE), lambda i: (0, i)), _const_spec((ROPE_HALF, 1))],
        out_specs=[pl.BlockSpec((ROPE_HALF, TABLE_TILE), lambda i: (0, i))] * 2,
        out_shape=[out] * 2,
        compiler_params=_params("arbitrary"),
        name="rope_tables",
    )(positions.reshape(1, t), inv.reshape(ROPE_HALF, 1))


def _t5_large_thresholds():
    n = T5_BUCKETS // 2
    max_exact = n // 2
    assert T5_MAX_DIST == 16 * max_exact and n - max_exact == 8
    out = []
    for j in range(1, n - max_exact):
        a = max_exact
        while a * a < max_exact * max_exact * 2 ** j:
            a += 1
        out.append(a)
    return out


def _t5_bias_kernel(table_ref, bias_ref):
    t = bias_ref.shape[-1]
    off = pl.program_id(0) * t
    row = lax.broadcasted_iota(jnp.int32, (t, t), 0)
    col = lax.broadcasted_iota(jnp.int32, (t, t), 1)
    rel = row - col - off
    n = T5_BUCKETS // 2
    max_exact = n // 2
    a = jnp.abs(rel)
    large = jnp.full((t, t), max_exact, jnp.int32)
    for thr in _t5_large_thresholds():
        large = large + (a >= thr).astype(jnp.int32)
    bucket = jnp.where(rel > 0, n, 0) + jnp.where(a < max_exact, a, large)
    far_bucket = n - 1
    for h in range(DIFF_HEADS):
        val = jnp.full((t, t), table_ref[0, h], jnp.float32)
        for b in range(1, T5_BUCKETS):
            val = jnp.where(bucket == b, table_ref[b, h], val)
        bias_ref[0, h] = (val - table_ref[far_bucket, h]) * LOG2E


def _t5_bias_tiles(t5_table):
    t = ATTN_TILE
    assert t >= _t5_large_thresholds()[-1]
    return pl.pallas_call(
        _t5_bias_kernel,
        grid=(2,),
        in_specs=[pl.BlockSpec(memory_space=pltpu.SMEM)],
        out_specs=pl.BlockSpec((1, DIFF_HEADS, t, t), lambda i: (i, 0, 0, 0)),
        out_shape=jax.ShapeDtypeStruct((2, DIFF_HEADS, t, t), jnp.float32),
        compiler_params=_params("arbitrary"),
        name="t5_bias_tiles",
    )(t5_table.astype(jnp.float32))


def _mem_kv_kernel(mem_ref, g_ref, wk_ref, wvt_ref, ones_ref, gk_ref, km_ref, vmt_ref):
    x = mem_ref[0]
    h = _bf16(x * _rms_scale(x, D_MODEL) * g_ref[0])
    k = _dot(h, wk_ref[0])
    for hd in range(MEM_HEADS):
        sl = slice(hd * MEM_HEAD_DIM, (hd + 1) * MEM_HEAD_DIM)
        kh = k[:, sl]
        km_ref[0, 0, :, sl] = _bf16(kh * _rms_scale(kh, MEM_HEAD_DIM) * gk_ref[0])
    vmt_ref[0, 0] = _bf16(_dot_nt(wvt_ref[0], h) + ones_ref[...])


def _mem_kv(mem, g_mem, w_mem_kv, g_mem_k):
    b, m, _ = mem.shape
    width = MEM_HEADS * MEM_HEAD_DIM
    rows = MEM_HEADS * MEM_VROWS
    w_k = _bf16(w_mem_kv[:, :, :width])
    w_vt = _bf16(_transposed_value_weight(
        w_mem_kv[:, :, width:].reshape(DEPTH, D_MODEL, MEM_HEADS, MEM_HEAD_DIM), MEM_VROWS))
    ones = _ones_rows(MEM_HEADS, MEM_VROWS, MEM_HEAD_DIM)
    layer = lambda l, i: (l, 0, 0)
    return pl.pallas_call(
        _mem_kv_kernel,
        grid=(DEPTH, b),
        in_specs=[
            pl.BlockSpec((1, m, D_MODEL), lambda l, i: (i, 0, 0)),
            pl.BlockSpec((1, 1, D_MODEL), layer),
            pl.BlockSpec((1, D_MODEL, width), layer),
            pl.BlockSpec((1, rows, D_MODEL), layer),
            _const_spec(ones.shape),
            pl.BlockSpec((1, 1, MEM_HEAD_DIM), layer),
        ],
        out_specs=[pl.BlockSpec((1, 1, m, width), lambda l, i: (l, i, 0, 0)),
                   pl.BlockSpec((1, 1, rows, m), lambda l, i: (l, i, 0, 0))],
        out_shape=[jax.ShapeDtypeStruct((DEPTH, b, m, width), jnp.bfloat16),
                   jax.ShapeDtypeStruct((DEPTH, b, rows, m), jnp.bfloat16)],
        compiler_params=_params("arbitrary", "arbitrary"),
        name="mem_kv",
    )(mem, g_mem.reshape(DEPTH, 1, D_MODEL), w_k, w_vt, ones, g_mem_k.reshape(DEPTH, 1, MEM_HEAD_DIM))


def _rope_rows(x1, x2, cos_t, sin_t):
    return x1 * cos_t - x2 * sin_t, x2 * cos_t + x1 * sin_t


def _proj_kernel(x_ref, gmix_ref, wt_ref, gcq_ref, wuqt_ref, gckv_ref, wukt_ref, wuvt_ref, vones_ref,
                 dvones_ref, gq_ref, gk_ref, qpad_ref, kpad_ref, cos_ref, sin_ref, gdq_ref, gdk_ref, gmq_ref,
                 km_ref, vmt_ref,
                 qt_ref, k_ref, vt_ref, dq0t_ref, dq1t_ref, dk_ref, dvt_ref, oc_ref):
    x = x_ref[...]
    tm = x.shape[0]
    h = _bf16(x * _rms_scale(x, D_MODEL) * gmix_ref[...])
    zt = _dot_nt(wt_ref[...], h)
    cos_t, sin_t = cos_ref[...], sin_ref[...]

    c_q = zt[_R_CQ:_R_CKV]
    cq_n = _bf16(c_q * _col_scale(_sumsq(c_q), Q_LORA) * gcq_ref[...])
    qt_raw = _dot(wuqt_ref[...], cq_n)
    c_kv = zt[_R_CKV:_R_KR]
    ckv_n = _bf16(c_kv * _col_scale(_sumsq(c_kv), KV_LORA) * gckv_ref[...])
    k_nope = _dot(wukt_ref[...], ckv_n)
    vt_ref[...] = _bf16(_dot(wuvt_ref[...], ckv_n) + vones_ref[...])
    gq, gk = gq_ref[...], gk_ref[...]
    k_r = zt[_R_KR:_R_DQ]
    kr_ss = _sumsq(k_r)
    kr_g = k_r * gk[MLA_NOPE:]
    kr1, kr2 = _rope_rows(kr_g[:ROPE_HALF], kr_g[ROPE_HALF:], cos_t, sin_t)
    q_pad = jnp.broadcast_to(qpad_ref[...], (LANES - MLA_QK, tm))
    k_pad = jnp.broadcast_to(kpad_ref[...], (LANES - MLA_QK, tm))
    for hd in range(MLA_HEADS):
        rows = slice(hd * LANES, (hd + 1) * LANES)
        q = qt_raw[hd * MLA_QK:(hd + 1) * MLA_QK]
        qn = q * _col_scale(_sumsq(q), MLA_QK) * gq
        q1, q2 = _rope_rows(qn[MLA_NOPE:MLA_NOPE + ROPE_HALF], qn[MLA_NOPE + ROPE_HALF:], cos_t, sin_t)
        qt_ref[rows, :] = _bf16(jnp.concatenate([qn[:MLA_NOPE], q1, q2, q_pad], axis=0))
        kn = k_nope[hd * MLA_NOPE:(hd + 1) * MLA_NOPE]
        rs = _col_scale(_sumsq(kn) + kr_ss, MLA_QK)
        kt = jnp.concatenate([kn * rs * gk[:MLA_NOPE], kr1 * rs, kr2 * rs, k_pad], axis=0)
        k_ref[:, rows] = _bf16(kt.T)

    gdq, gdk = gdq_ref[...], gdk_ref[...]
    zero_half = jnp.zeros((DIFF_QK, tm), jnp.float32)

    def half_norm(t, g):
        return t * _col_scale(_sumsq(t), DIFF_QK) * g

    for hd in range(DIFF_HEADS):
        rows = slice(hd * LANES, (hd + 1) * LANES)
        r0 = _R_DQ + hd * LANES
        c0 = half_norm(zt[r0:r0 + DIFF_QK], gdq)
        c1 = half_norm(zt[r0 + DIFF_QK:r0 + LANES], gdq)
        dq0t_ref[rows, :] = _bf16(jnp.concatenate([c0, zero_half], axis=0))
        dq1t_ref[rows, :] = _bf16(jnp.concatenate([zero_half, c1], axis=0))
        r0 = _R_DK + hd * LANES
        k0, k1 = half_norm(zt[r0:r0 + DIFF_QK], gdk), half_norm(zt[r0 + DIFF_QK:r0 + LANES], gdk)
        dk_ref[:, rows] = _bf16(jnp.concatenate([k0, k1], axis=0).T)
    dvt_ref[...] = _bf16(zt[_R_DV:_R_MQ] + dvones_ref[...])

    gmq = gmq_ref[...]
    for hd in range(MEM_HEADS):
        sl = slice(hd * MEM_HEAD_DIM, (hd + 1) * MEM_HEAD_DIM)
        mq = zt[_R_MQ + hd * MEM_HEAD_DIM:_R_MQ + (hd + 1) * MEM_HEAD_DIM]
        mq = _bf16(mq * _col_scale(_sumsq(mq), MEM_HEAD_DIM) * gmq)
        s = _dot(km_ref[0, 0, :, sl], mq)
        p = _bf16(jnp.exp(s - jnp.max(s, axis=0, keepdims=True)))
        o = _dot(vmt_ref[0, 0, hd * MEM_VROWS:(hd + 1) * MEM_VROWS, :], p)
        o = o[:MEM_HEAD_DIM] / o[MEM_HEAD_DIM:MEM_HEAD_DIM + 1]
        oc_ref[:, sl] = _bf16(o.T)


def _proj(layer, x2, w, tabs, km, vmt, seq):
    t = x2.shape[0]
    tm = PROJ_TILE
    steps_per_batch = seq // tm
    tok = lambda c: pl.BlockSpec((tm, c), lambda i: (i, 0))
    tokt = lambda r: pl.BlockSpec((r, tm), lambda i: (0, i))
    mem_map = lambda i: (layer, i // steps_per_batch, 0, 0)
    bf = jnp.bfloat16
    stacked = lambda name: (w[name], _layer_spec(w[name], layer))
    fixed = lambda a: (a, _const_spec(a.shape))
    consts = [stacked("g_mix"), stacked("w_t"), stacked("g_cq"), stacked("w_uqt"), stacked("g_ckv"),
              stacked("w_ukt"), stacked("w_uvt"), fixed(_ones_rows(MLA_HEADS, MLA_VROWS, MLA_V)),
              fixed(_ones_rows(DIFF_HEADS, DIFF_VROWS, DIFF_V)), stacked("g_q"), stacked("g_k"),
              stacked("q_pad"), fixed(_ones_rows(1, LANES - MLA_QK, 0))]
    gains = [stacked("g_dq"), stacked("g_dk"), stacked("g_mq")]
    wide, vt_rows, dvt_rows = MLA_HEADS * LANES, MLA_HEADS * MLA_VROWS, DIFF_HEADS * DIFF_VROWS
    outs = [(tokt(wide), (wide, t)), (tok(wide), (t, wide)), (tokt(vt_rows), (vt_rows, t)),
            (tokt(BRANCH_W), (BRANCH_W, t)), (tokt(BRANCH_W), (BRANCH_W, t)), (tok(BRANCH_W), (t, BRANCH_W)),
            (tokt(dvt_rows), (dvt_rows, t)), (tok(BRANCH_W), (t, BRANCH_W))]
    return pl.pallas_call(
        _proj_kernel,
        grid=(t // tm,),
        in_specs=([tok(D_MODEL)] + [spec for _, spec in consts] + [tokt(ROPE_HALF)] * 2
                  + [spec for _, spec in gains]
                  + [pl.BlockSpec((1, 1) + km.shape[2:], mem_map), pl.BlockSpec((1, 1) + vmt.shape[2:], mem_map)]),
        out_specs=[spec for spec, _ in outs],
        out_shape=[jax.ShapeDtypeStruct(shape, bf) for _, shape in outs],
        compiler_params=_params("arbitrary"),
        name="proj",
    )(x2, *[a for a, _ in consts], *tabs, *[a for a, _ in gains], km, vmt)


def _chunk_mask_t(shape):
    krow = lax.broadcasted_iota(jnp.int32, shape, 0)
    qcol = lax.broadcasted_iota(jnp.int32, shape, 1)
    return (krow // CHUNK) <= (qcol // CHUNK)


def _flash_step(n, score_fn, vt_fn, m_ref, acc_ref, online):
    def scores_and_max(i):
        s = score_fn(i)
        if not online:
            return s, None, None
        m_old = m_ref[i]
        m_new = jnp.maximum(m_old, jnp.max(s, axis=0, keepdims=True))
        m_ref[i] = m_new
        return s, m_new, jnp.exp2(m_old - m_new)

    def exponentials(a):
        s, m_new, alpha = a
        return _bf16(jnp.exp2(s if m_new is None else s - m_new)), alpha

    def value_product(i, b):
        p, alpha = b
        acc = acc_ref[i] if alpha is None else alpha * acc_ref[i]
        acc_ref[i] = acc + _dot(vt_fn(i), p)

    stage_a, stage_b = {}, {}
    for it in range(n + 2):
        if it < n:
            stage_a[it] = scores_and_max(it)
        if 0 <= it - 1 < n:
            stage_b[it - 1] = exponentials(stage_a.pop(it - 1))
        if 0 <= it - 2 < n:
            value_product(it - 2, stage_b.pop(it - 2))


def _softmax_modes(safe, m_ref, acc_ref, run):
    acc_ref[...] = jnp.zeros(acc_ref.shape, jnp.float32)

    @pl.when(safe == 1)
    def _():
        run(False)

    @pl.when(safe != 1)
    def _():
        m_ref[...] = jnp.full(m_ref.shape, NEG, jnp.float32)
        run(True)


def _key_tile(start_tile, n_tiles, ta):
    return pl.ds(pl.multiple_of(start_tile * ta, ta), n_tiles * ta)


def _for_unmasked_tiles(n, tile):
    def body(i, carry):
        tile(i * FAR_TILES, FAR_TILES)
        return carry

    n = jnp.maximum(n, 0)
    lax.fori_loop(0, n // FAR_TILES, body, 0)
    for r in range(1, FAR_TILES):
        @pl.when(n % FAR_TILES >= r)
        def _():
            tile(n - n % FAR_TILES + (r - 1), 1)


def _mla_attn_kernel(safe_ref, qt_ref, k_ref, vt_ref, o_ref, m_ref, acc_ref, *, layer):
    qi = pl.program_id(1)
    ta = qt_ref.shape[1]

    def run(online):
        def tile(start, n_tiles, masked=False):
            keys = _key_tile(start, n_tiles, ta)
            mask = _chunk_mask_t((ta, ta)) if masked else None

            def score(hd):
                sl = slice(hd * LANES, (hd + 1) * LANES)
                s = _dot(k_ref[keys, sl], qt_ref[sl, :])
                return jnp.where(mask, s, NEG) if masked else s

            _flash_step(MLA_HEADS, score,
                        lambda hd: vt_ref[hd * MLA_VROWS:(hd + 1) * MLA_VROWS, keys], m_ref, acc_ref, online)

        _for_unmasked_tiles(qi, tile)
        tile(qi, 1, masked=True)

    _softmax_modes(safe_ref[layer], m_ref, acc_ref, run)
    outs = []
    for hd in range(MLA_HEADS):
        a = acc_ref[hd]
        outs.append(a[:MLA_V] / a[MLA_V:MLA_V + 1])
    o_ref[...] = _bf16(jnp.concatenate(outs, axis=0).T)


def _mla_attn(layer, safe, qt, k, vt, batch, seq):
    ta = ATTN_TILE
    nb = seq // ta
    return pl.pallas_call(
        functools.partial(_mla_attn_kernel, layer=layer),
        grid=(batch, nb),
        in_specs=[pl.BlockSpec(memory_space=pltpu.SMEM),
                  pl.BlockSpec((MLA_HEADS * LANES, ta), lambda b, qi: (0, b * nb + qi)),
                  pl.BlockSpec((seq, MLA_HEADS * LANES), lambda b, qi: (b, 0)),
                  pl.BlockSpec((MLA_HEADS * MLA_VROWS, seq), lambda b, qi: (0, b))],
        out_specs=pl.BlockSpec((ta, BRANCH_W), lambda b, qi: (b * nb + qi, 0)),
        out_shape=jax.ShapeDtypeStruct((batch * seq, BRANCH_W), jnp.bfloat16),
        scratch_shapes=[pltpu.VMEM((MLA_HEADS, 1, ta), jnp.float32),
                        pltpu.VMEM((MLA_HEADS, MLA_VROWS, ta), jnp.float32)],
        compiler_params=_params("arbitrary", "arbitrary"),
        name="mla_attn",
    )(safe, qt, k, vt)


def _diff_attn_kernel(safe_ref, ref_ref, lam_ref, q0t_ref, q1t_ref, k_ref, vt_ref,
                      bias_ref, gout_ref, o_ref, m_ref, acc_ref, *, layer, out_scale):
    qi = pl.program_id(1)
    ta = q0t_ref.shape[1]

    def run(online):
        def tile(start, n_tiles, bias_tile=None, masked=False):
            keys = _key_tile(start, n_tiles, ta)
            mask = _chunk_mask_t((ta, ta)) if masked else None

            def score(idx):
                hd, qt_ref = idx // 2, (q0t_ref, q1t_ref)[idx % 2]
                sl = slice(hd * LANES, (hd + 1) * LANES)
                s = _dot(k_ref[keys, sl], qt_ref[sl, :])
                if bias_tile is not None:
                    s = s + bias_ref[bias_tile, hd]
                if not online:
                    s = s - ref_ref[layer * DIFF_HEADS + hd]
                return jnp.where(mask, s, NEG) if masked else s

            _flash_step(2 * DIFF_HEADS, score,
                        lambda idx: vt_ref[(idx // 2) * DIFF_VROWS:(idx // 2 + 1) * DIFF_VROWS, keys],
                        m_ref, acc_ref, online)

        _for_unmasked_tiles(qi - 1, tile)

        @pl.when(qi > 0)
        def _():
            tile(qi - 1, 1, bias_tile=1)

        tile(qi, 1, bias_tile=0, masked=True)

    _softmax_modes(safe_ref[layer], m_ref, acc_ref, run)
    lam = lam_ref[layer]
    g = gout_ref[...] * out_scale
    outs = []
    for hd in range(DIFF_HEADS):
        a0, a1 = acc_ref[2 * hd], acc_ref[2 * hd + 1]
        o = a0[:DIFF_V] / a0[DIFF_V:DIFF_V + 1] - lam * (a1[:DIFF_V] / a1[DIFF_V:DIFF_V + 1])
        outs.append(o * _col_scale(_sumsq(o), DIFF_V) * g)
    o_ref[...] = _bf16(jnp.concatenate(outs, axis=0).T)


def _diff_attn(layer, safe, ref, lam, q0t, q1t, k, vt, bias, g_out, out_scale, batch, seq):
    ta = ATTN_TILE
    nb = seq // ta
    qt_spec = pl.BlockSpec((BRANCH_W, ta), lambda b, qi: (0, b * nb + qi))
    smem = pl.BlockSpec(memory_space=pltpu.SMEM)
    return pl.pallas_call(
        functools.partial(_diff_attn_kernel, layer=layer, out_scale=out_scale),
        grid=(batch, nb),
        in_specs=[smem, smem, smem, qt_spec, qt_spec,
                  pl.BlockSpec((seq, BRANCH_W), lambda b, qi: (b, 0)),
                  pl.BlockSpec((DIFF_HEADS * DIFF_VROWS, seq), lambda b, qi: (0, b)),
                  _single_spec(bias), _layer_spec(g_out, layer, single=True)],
        out_specs=pl.BlockSpec((ta, BRANCH_W), lambda b, qi: (b * nb + qi, 0)),
        out_shape=jax.ShapeDtypeStruct((batch * seq, BRANCH_W), jnp.bfloat16),
        scratch_shapes=[pltpu.VMEM((2 * DIFF_HEADS, 1, ta), jnp.float32),
                        pltpu.VMEM((2 * DIFF_HEADS, DIFF_VROWS, ta), jnp.float32)],
        compiler_params=_params("arbitrary", "arbitrary"),
        name="diff_attn",
    )(safe, ref, lam, q0t, q1t, k, vt, bias, g_out)


def _merge_mlp_kernel(x_ref, oa_ref, ob_ref, oc_ref, gmix_ref, wg_ref, wb_ref, wout_ref,
                      gmlp_ref, w1_ref, w2_ref, out_ref):
    x = x_ref[...]
    h = _bf16(x * _rms_scale(x, D_MODEL) * gmix_ref[...])
    y = None
    for n, o_ref in enumerate((oa_ref, ob_ref, oc_ref)):
        gate = 1.0 / (1.0 + jnp.exp(-_dot(h, wg_ref[:, n * D_MODEL:(n + 1) * D_MODEL])))
        term = gate * _dot(o_ref[...], wb_ref[n])
        y = term if y is None else y + term
    x = x + _dot(_bf16(y), wout_ref[...])
    h2 = _bf16(x * _rms_scale(x, D_MODEL) * gmlp_ref[...])
    for c in range(D_FF // FF_CHUNK):
        f = jnp.maximum(_dot(h2, w1_ref[:, c * FF_CHUNK:(c + 1) * FF_CHUNK]), 0.0)
        x = x + _dot(_bf16(f * f), w2_ref[c * FF_CHUNK:(c + 1) * FF_CHUNK, :])
    out_ref[...] = x


def _merge_mlp(layer, x2, o_a, o_b, o_c, w):
    t = x2.shape[0]
    tm = TOKEN_TILE
    tok = lambda c: pl.BlockSpec((tm, c), lambda i: (i, 0))
    consts = [w["g_mix"], w["w_gate"], w["w_branch"], w["w_out"], w["g_mlp"], w["w_ff1"], w["w_ff2"]]
    return pl.pallas_call(
        _merge_mlp_kernel,
        grid=(t // tm,),
        in_specs=([tok(D_MODEL)] + [tok(BRANCH_W)] * 3
                  + [_layer_spec(c, layer, single=True) for c in consts]),
        out_specs=tok(D_MODEL),
        out_shape=jax.ShapeDtypeStruct((t, D_MODEL), jnp.float32),
        compiler_params=_params("arbitrary"),
        name="merge_mlp",
    )(x2, o_a, o_b, o_c, *consts)


def _transposed_value_weight(w, rows):
    depth, n_in, heads, width = w.shape
    wt = jnp.pad(jnp.transpose(w, (0, 2, 3, 1)), ((0, 0), (0, 0), (0, rows - width), (0, 0)))
    return wt.reshape(depth, heads * rows, n_in)


def _ones_rows(heads, rows, at):
    col = np.zeros((heads, rows, 1), np.float32)
    col[:, at] = 1.0
    return jnp.asarray(col.reshape(heads * rows, 1))


def _layer_weights(w_in, g_mix, g_cq, w_uq, g_ckv, w_ukv, g_mla_q, g_mla_k, g_diff_q, g_diff_k,
                   g_mem_q, w_branch, w_out, g_mlp, w_ff1, w_ff2):
    splits = np.cumsum([Q_LORA, KV_LORA, MLA_ROPE, DIFF_HEADS * 2 * DIFF_QK, DIFF_HEADS * 2 * DIFF_QK,
                        DIFF_HEADS * DIFF_V, MEM_HEADS * MEM_HEAD_DIM]).tolist()
    c_dv, c_mq, c_gate = splits[4], splits[5], splits[6]
    t_in = jnp.swapaxes(w_in[:, :, :c_gate], 1, 2)
    w_dvt = _transposed_value_weight(
        w_in[:, :, c_dv:c_mq].reshape(DEPTH, D_MODEL, DIFF_HEADS, DIFF_V), DIFF_VROWS)
    w_t = jnp.concatenate([t_in[:, :c_dv], w_dvt, t_in[:, c_mq:]], axis=1)
    assert w_t.shape[1] == _R_END
    w_uqt = _transposed_value_weight(w_uq.reshape(DEPTH, Q_LORA, MLA_HEADS, MLA_QK), MLA_QK)
    w_ukv = w_ukv.reshape(DEPTH, KV_LORA, MLA_HEADS, MLA_NOPE + MLA_V)
    w_ukt = _transposed_value_weight(w_ukv[..., :MLA_NOPE], MLA_NOPE)
    w_uvt = _transposed_value_weight(w_ukv[..., MLA_NOPE:], MLA_VROWS)
    row = lambda g: g[:, None, :]
    col = lambda g: g[:, :, None]
    return {
        "g_mix": row(g_mix), "w_t": _bf16(w_t), "g_cq": col(g_cq), "w_uqt": _bf16(w_uqt),
        "g_ckv": col(g_ckv), "w_ukt": _bf16(w_ukt), "w_uvt": _bf16(w_uvt),
        "g_q": col(g_mla_q * (MLA_QK ** -0.5 * LOG2E)), "g_k": col(g_mla_k),
        "g_dq": col(g_diff_q * (DIFF_QK ** -0.5 * LOG2E)), "g_dk": col(g_diff_k),
        "g_mq": col(g_mem_q * MEM_HEAD_DIM ** -0.5),
        "w_gate": _bf16(w_in[:, :, c_gate:]), "w_branch": _bf16(w_branch), "w_out": _bf16(w_out),
        "g_mlp": row(g_mlp), "w_ff1": _bf16(w_ff1), "w_ff2": _bf16(w_ff2),
    }


def _logit_bound(g_q, g_k, n, scale):
    bound = (n * scale * LOG2E * BOUND_MARGIN) * jnp.max(jnp.abs(g_q), axis=-1) * jnp.max(jnp.abs(g_k), axis=-1)
    return _bf16(bound).astype(jnp.float32)


def _softmax_bounds(g_mla_q, g_mla_k, g_diff_q, g_diff_k, bias):
    r_mla = _logit_bound(g_mla_q, g_mla_k, MLA_QK, MLA_QK ** -0.5)
    bias_hi = jnp.max(bias, axis=(0, 2, 3))
    bias_range = jnp.max(bias_hi - jnp.min(bias, axis=(0, 2, 3)))
    r_diff = _logit_bound(g_diff_q, g_diff_k, DIFF_QK, DIFF_QK ** -0.5)
    safe = lambda spread: (spread <= SAFE_LOGIT_SPREAD).astype(jnp.int32)
    return (r_mla, safe(2.0 * r_mla),
            (r_diff[:, None] + bias_hi[None, :]).reshape(-1), safe(2.0 * r_diff + bias_range))


def kernel(x, mem, positions, t5_table, g_mix, g_mem, w_in, g_cq, w_uq, g_ckv, w_ukv, g_mla_q, g_mla_k, g_diff_q, g_diff_k, lam_q1, lam_k1, lam_q2, lam_k2, g_diff_out, w_mem_kv, g_mem_q, g_mem_k, w_branch, w_out, g_mlp, w_ff1, w_ff2):
    b, s, _ = x.shape
    assert s % ATTN_TILE == 0 and s % TOKEN_TILE == 0 and s % PROJ_TILE == 0 and (b * s) % TABLE_TILE == 0
    tabs = _rope_tables(positions)
    bias = _t5_bias_tiles(t5_table)
    km, vmt = _mem_kv(mem, g_mem, w_mem_kv, g_mem_k)
    weights = _layer_weights(w_in, g_mix, g_cq, w_uq, g_ckv, w_ukv, g_mla_q, g_mla_k, g_diff_q,
                             g_diff_k, g_mem_q, w_branch, w_out, g_mlp, w_ff1, w_ff2)
    f32 = jnp.float32
    lam_init = [0.8 - 0.6 * math.exp(-0.3 * l) for l in range(DEPTH)]
    lam = (jnp.exp(jnp.sum(lam_q1.astype(f32) * lam_k1.astype(f32), axis=-1))
           - jnp.exp(jnp.sum(lam_q2.astype(f32) * lam_k2.astype(f32), axis=-1))
           + jnp.asarray(lam_init, f32))
    x2 = x.reshape(b * s, D_MODEL)
    g_out = g_diff_out[:, :, None]
    r_mla, safe_mla, r_diff, safe_diff = _softmax_bounds(g_mla_q, g_mla_k, g_diff_q, g_diff_k, bias)
    weights["q_pad"] = jnp.pad(-r_mla[:, None, None], ((0, 0), (0, LANES - MLA_QK - 1), (0, 0)))
    for l in range(DEPTH):
        qt, k, vt, dq0t, dq1t, dk, dvt, o_c = _proj(l, x2, weights, tabs, km, vmt, s)
        o_a = _mla_attn(l, safe_mla, qt, k, vt, b, s)
        o_b = _diff_attn(l, safe_diff, r_diff, lam, dq0t, dq1t, dk, dvt, bias, g_out,
                         1.0 - lam_init[l], b, s)
        x2 = _merge_mlp(l, x2, o_a, o_b, o_c, weights)
    return x2.reshape(b, s, D_MODEL)
```

```python
import functools
import math

import jax
import jax.numpy as jnp
import numpy as np
from jax import lax
from jax.experimental import pallas as pl
from jax.experimental.pallas import tpu as pltpu

D_MODEL = 1024
DEPTH = 4
CHUNK = 64
MLA_HEADS = 8
MLA_NOPE = 64
MLA_ROPE = 32
MLA_V = 64
MLA_QK = MLA_NOPE + MLA_ROPE
Q_LORA = 384
KV_LORA = 256
ROPE_BASE = 10000.0
DIFF_HEADS = 4
DIFF_QK = 64
DIFF_V = 2 * DIFF_QK
MEM_HEADS = 4
MEM_HEAD_DIM = 128
BRANCH_W = 512
N_BRANCH = 3
D_FF = 4 * D_MODEL
T5_BUCKETS = 32
T5_MAX_DIST = 128
EPS = 1e-6
NEG = -1e30

LANES = 128
BF16_SUBLANES = 16
V7X_VMEM_BYTES = 64 * 1024 * 1024
VMEM_LIMIT = V7X_VMEM_BYTES * 7 // 8

TOKEN_TILE = 512
PROJ_TILE = 1024
ATTN_TILE = 512
FAR_TILES = 2
TABLE_TILE = 4096
FF_CHUNK = 1024
ROPE_HALF = MLA_ROPE // 2


def _round_up(n, m):
    return -(-n // m) * m


MLA_VROWS = _round_up(MLA_V + 1, BF16_SUBLANES)
DIFF_VROWS = _round_up(DIFF_V + 1, BF16_SUBLANES)
MEM_VROWS = _round_up(MEM_HEAD_DIM + 1, BF16_SUBLANES)
LOG2E = math.log2(math.e)

SAFE_LOGIT_SPREAD = 120.0
BOUND_MARGIN = 1.02

_R_CQ = 0
_R_CKV = _R_CQ + Q_LORA
_R_KR = _R_CKV + KV_LORA
_R_DQ = _R_KR + MLA_ROPE
_R_DK = _R_DQ + DIFF_HEADS * LANES
_R_DV = _R_DK + DIFF_HEADS * LANES
_R_MQ = _R_DV + DIFF_HEADS * DIFF_VROWS
_R_END = _R_MQ + MEM_HEADS * MEM_HEAD_DIM


def _bf16(x):
    return x.astype(jnp.bfloat16)


def _dot(a, b):
    return jnp.dot(a, b, preferred_element_type=jnp.float32)


def _dot_nt(a, b):
    return lax.dot_general(a, b, (((1,), (1,)), ((), ())), preferred_element_type=jnp.float32)


def _rms_scale(x, n):
    return lax.rsqrt(jnp.sum(x * x, axis=-1, keepdims=True) * (1.0 / n) + EPS)


def _sumsq(x):
    return jnp.sum(x * x, axis=0, keepdims=True)


def _col_scale(sumsq, n):
    return lax.rsqrt(sumsq * (1.0 / n) + EPS)


def _const_spec(shape):
    return pl.BlockSpec(shape, lambda *_: (0,) * len(shape))


def _single_spec(a):
    return pl.BlockSpec(a.shape, lambda *_: (0,) * a.ndim, pipeline_mode=pl.Buffered(1))


def _layer_spec(a, layer, single=False):
    return pl.BlockSpec((None,) + a.shape[1:], lambda *_: (layer,) + (0,) * (a.ndim - 1),
                        pipeline_mode=pl.Buffered(1) if single else None)


def _params(*sem):
    return pltpu.CompilerParams(dimension_semantics=sem, vmem_limit_bytes=VMEM_LIMIT)


def _rope_tables_kernel(pos_ref, inv_ref, cos_ref, sin_ref):
    ang = inv_ref[...] * pos_ref[...].astype(jnp.float32)
    cos_ref[...] = jnp.cos(ang)
    sin_ref[...] = jnp.sin(ang)


def _rope_tables(positions):
    t = positions.size
    inv = jnp.power(jnp.float32(ROPE_BASE), -jnp.arange(ROPE_HALF, dtype=jnp.float32) / ROPE_HALF)
    out = jax.ShapeDtypeStruct((ROPE_HALF, t), jnp.float32)
    return pl.pallas_call(
        _rope_tables_kernel,
        grid=(t // TABLE_TILE,),
        in_specs=[pl.BlockSpec((1, TABLE_TILE), lambda i: (0, i)), _const_spec((ROPE_HALF, 1))],
        out_specs=[pl.BlockSpec((ROPE_HALF, TABLE_TILE), lambda i: (0, i))] * 2,
        out_shape=[out] * 2,
        compiler_params=_params("arbitrary"),
        name="rope_tables",
    )(positions.reshape(1, t), inv.reshape(ROPE_HALF, 1))


def _t5_large_thresholds():
    n = T5_BUCKETS // 2
    max_exact = n // 2
    assert T5_MAX_DIST == 16 * max_exact and n - max_exact == 8
    out = []
    for j in range(1, n - max_exact):
        a = max_exact
        while a * a < max_exact * max_exact * 2 ** j:
            a += 1
        out.append(a)
    return out


def _t5_bias_kernel(table_ref, bias_ref):
    t = bias_ref.shape[-1]
    off = pl.program_id(0) * t
    row = lax.broadcasted_iota(jnp.int32, (t, t), 0)
    col = lax.broadcasted_iota(jnp.int32, (t, t), 1)
    rel = row - col - off
    n = T5_BUCKETS // 2
    max_exact = n // 2
    a = jnp.abs(rel)
    large = jnp.full((t, t), max_exact, jnp.int32)
    for thr in _t5_large_thresholds():
        large = large + (a >= thr).astype(jnp.int32)
    bucket = jnp.where(rel > 0, n, 0) + jnp.where(a < max_exact, a, large)
    far_bucket = n - 1
    for h in range(DIFF_HEADS):
        val = jnp.full((t, t), table_ref[0, h], jnp.float32)
        for b in range(1, T5_BUCKETS):
            val = jnp.where(bucket == b, table_ref[b, h], val)
        bias_ref[0, h] = (val - table_ref[far_bucket, h]) * LOG2E


def _t5_bias_tiles(t5_table):
    t = ATTN_TILE
    assert t >= _t5_large_thresholds()[-1]
    return pl.pallas_call(
        _t5_bias_kernel,
        grid=(2,),
        in_specs=[pl.BlockSpec(memory_space=pltpu.SMEM)],
        out_specs=pl.BlockSpec((1, DIFF_HEADS, t, t), lambda i: (i, 0, 0, 0)),
        out_shape=jax.ShapeDtypeStruct((2, DIFF_HEADS, t, t), jnp.float32),
        compiler_params=_params("arbitrary"),
        name="t5_bias_tiles",
    )(t5_table.astype(jnp.float32))


def _mem_kv_kernel(mem_ref, g_ref, wk_ref, wvt_ref, ones_ref, gk_ref, km_ref, vmt_ref):
    x = mem_ref[0]
    h = _bf16(x * _rms_scale(x, D_MODEL) * g_ref[0])
    k = _dot(h, wk_ref[0])
    for hd in range(MEM_HEADS):
        sl = slice(hd * MEM_HEAD_DIM, (hd + 1) * MEM_HEAD_DIM)
        kh = k[:, sl]
        km_ref[0, 0, :, sl] = _bf16(kh * _rms_scale(kh, MEM_HEAD_DIM) * gk_ref[0])
    vmt_ref[0, 0] = _bf16(_dot_nt(wvt_ref[0], h) + ones_ref[...])


def _mem_kv(mem, g_mem, w_mem_kv, g_mem_k):
    b, m, _ = mem.shape
    width = MEM_HEADS * MEM_HEAD_DIM
    rows = MEM_HEADS * MEM_VROWS
    w_k = _bf16(w_mem_kv[:, :, :width])
    w_vt = _bf16(_transposed_value_weight(
        w_mem_kv[:, :, width:].reshape(DEPTH, D_MODEL, MEM_HEADS, MEM_HEAD_DIM), MEM_VROWS))
    ones = _ones_rows(MEM_HEADS, MEM_VROWS, MEM_HEAD_DIM)
    layer = lambda l, i: (l, 0, 0)
    return pl.pallas_call(
        _mem_kv_kernel,
        grid=(DEPTH, b),
        in_specs=[
            pl.BlockSpec((1, m, D_MODEL), lambda l, i: (i, 0, 0)),
            pl.BlockSpec((1, 1, D_MODEL), layer),
            pl.BlockSpec((1, D_MODEL, width), layer),
            pl.BlockSpec((1, rows, D_MODEL), layer),
            _const_spec(ones.shape),
            pl.BlockSpec((1, 1, MEM_HEAD_DIM), layer),
        ],
        out_specs=[pl.BlockSpec((1, 1, m, width), lambda l, i: (l, i, 0, 0)),
                   pl.BlockSpec((1, 1, rows, m), lambda l, i: (l, i, 0, 0))],
        out_shape=[jax.ShapeDtypeStruct((DEPTH, b, m, width), jnp.bfloat16),
                   jax.ShapeDtypeStruct((DEPTH, b, rows, m), jnp.bfloat16)],
        compiler_params=_params("arbitrary", "arbitrary"),
        name="mem_kv",
    )(mem, g_mem.reshape(DEPTH, 1, D_MODEL), w_k, w_vt, ones, g_mem_k.reshape(DEPTH, 1, MEM_HEAD_DIM))


def _rope_rows(x1, x2, cos_t, sin_t):
    return x1 * cos_t - x2 * sin_t, x2 * cos_t + x1 * sin_t


def _proj_kernel(x_ref, gmix_ref, wt_ref, gcq_ref, wuqt_ref, gckv_ref, wukt_ref, wuvt_ref, vones_ref,
                 dvones_ref, gq_ref, gk_ref, qpad_ref, kpad_ref, cos_ref, sin_ref, gdq_ref, gdk_ref, gmq_ref,
                 km_ref, vmt_ref,
                 qt_ref, k_ref, vt_ref, dq0t_ref, dq1t_ref, dk_ref, dvt_ref, oc_ref):
    x = x_ref[...]
    tm = x.shape[0]
    h = _bf16(x * _rms_scale(x, D_MODEL) * gmix_ref[...])
    zt = _dot_nt(wt_ref[...], h)
    cos_t, sin_t = cos_ref[...], sin_ref[...]

    c_q = zt[_R_CQ:_R_CKV]
    cq_n = _bf16(c_q * _col_scale(_sumsq(c_q), Q_LORA) * gcq_ref[...])
    qt_raw = _dot(wuqt_ref[...], cq_n)
    c_kv = zt[_R_CKV:_R_KR]
    ckv_n = _bf16(c_kv * _col_scale(_sumsq(c_kv), KV_LORA) * gckv_ref[...])
    k_nope = _dot(wukt_ref[...], ckv_n)
    vt_ref[...] = _bf16(_dot(wuvt_ref[...], ckv_n) + vones_ref[...])
    gq, gk = gq_ref[...], gk_ref[...]
    k_r = zt[_R_KR:_R_DQ]
    kr_ss = _sumsq(k_r)
    kr_g = k_r * gk[MLA_NOPE:]
    kr1, kr2 = _rope_rows(kr_g[:ROPE_HALF], kr_g[ROPE_HALF:], cos_t, sin_t)
    q_pad = jnp.broadcast_to(qpad_ref[...], (LANES - MLA_QK, tm))
    k_pad = jnp.broadcast_to(kpad_ref[...], (LANES - MLA_QK, tm))
    for hd in range(MLA_HEADS):
        rows = slice(hd * LANES, (hd + 1) * LANES)
        q = qt_raw[hd * MLA_QK:(hd + 1) * MLA_QK]
        qn = q * _col_scale(_sumsq(q), MLA_QK) * gq
        q1, q2 = _rope_rows(qn[MLA_NOPE:MLA_NOPE + ROPE_HALF], qn[MLA_NOPE + ROPE_HALF:], cos_t, sin_t)
        qt_ref[rows, :] = _bf16(jnp.concatenate([qn[:MLA_NOPE], q1, q2, q_pad], axis=0))
        kn = k_nope[hd * MLA_NOPE:(hd + 1) * MLA_NOPE]
        rs = _col_scale(_sumsq(kn) + kr_ss, MLA_QK)
        kt = jnp.concatenate([kn * rs * gk[:MLA_NOPE], kr1 * rs, kr2 * rs, k_pad], axis=0)
        k_ref[:, rows] = _bf16(kt.T)

    gdq, gdk = gdq_ref[...], gdk_ref[...]
    zero_half = jnp.zeros((DIFF_QK, tm), jnp.float32)

    def half_norm(t, g):
        return t * _col_scale(_sumsq(t), DIFF_QK) * g

    for hd in range(DIFF_HEADS):
        rows = slice(hd * LANES, (hd + 1) * LANES)
        r0 = _R_DQ + hd * LANES
        c0 = half_norm(zt[r0:r0 + DIFF_QK], gdq)
        c1 = half_norm(zt[r0 + DIFF_QK:r0 + LANES], gdq)
        dq0t_ref[rows, :] = _bf16(jnp.concatenate([c0, zero_half], axis=0))
        dq1t_ref[rows, :] = _bf16(jnp.concatenate([zero_half, c1], axis=0))
        r0 = _R_DK + hd * LANES
        k0, k1 = half_norm(zt[r0:r0 + DIFF_QK], gdk), half_norm(zt[r0 + DIFF_QK:r0 + LANES], gdk)
        dk_ref[:, rows] = _bf16(jnp.concatenate([k0, k1], axis=0).T)
    dvt_ref[...] = _bf16(zt[_R_DV:_R_MQ] + dvones_ref[...])

    gmq = gmq_ref[...]
    for hd in range(MEM_HEADS):
        sl = slice(hd * MEM_HEAD_DIM, (hd + 1) * MEM_HEAD_DIM)
        mq = zt[_R_MQ + hd * MEM_HEAD_DIM:_R_MQ + (hd + 1) * MEM_HEAD_DIM]
        mq = _bf16(mq * _col_scale(_sumsq(mq), MEM_HEAD_DIM) * gmq)
        s = _dot(km_ref[0, 0, :, sl], mq)
        p = _bf16(jnp.exp(s - jnp.max(s, axis=0, keepdims=True)))
        o = _dot(vmt_ref[0, 0, hd * MEM_VROWS:(hd + 1) * MEM_VROWS, :], p)
        o = o[:MEM_HEAD_DIM] / o[MEM_HEAD_DIM:MEM_HEAD_DIM + 1]
        oc_ref[:, sl] = _bf16(o.T)


def _proj(layer, x2, w, tabs, km, vmt, seq):
    t = x2.shape[0]
    tm = PROJ_TILE
    steps_per_batch = seq // tm
    tok = lambda c: pl.BlockSpec((tm, c), lambda i: (i, 0))
    tokt = lambda r: pl.BlockSpec((r, tm), lambda i: (0, i))
    mem_map = lambda i: (layer, i // steps_per_batch, 0, 0)
    bf = jnp.bfloat16
    stacked = lambda name: (w[name], _layer_spec(w[name], layer))
    fixed = lambda a: (a, _const_spec(a.shape))
    consts = [stacked("g_mix"), stacked("w_t"), stacked("g_cq"), stacked("w_uqt"), stacked("g_ckv"),
              stacked("w_ukt"), stacked("w_uvt"), fixed(_ones_rows(MLA_HEADS, MLA_VROWS, MLA_V)),
              fixed(_ones_rows(DIFF_HEADS, DIFF_VROWS, DIFF_V)), stacked("g_q"), stacked("g_k"),
              stacked("q_pad"), fixed(_ones_rows(1, LANES - MLA_QK, 0))]
    gains = [stacked("g_dq"), stacked("g_dk"), stacked("g_mq")]
    wide, vt_rows, dvt_rows = MLA_HEADS * LANES, MLA_HEADS * MLA_VROWS, DIFF_HEADS * DIFF_VROWS
    outs = [(tokt(wide), (wide, t)), (tok(wide), (t, wide)), (tokt(vt_rows), (vt_rows, t)),
            (tokt(BRANCH_W), (BRANCH_W, t)), (tokt(BRANCH_W), (BRANCH_W, t)), (tok(BRANCH_W), (t, BRANCH_W)),
            (tokt(dvt_rows), (dvt_rows, t)), (tok(BRANCH_W), (t, BRANCH_W))]
    return pl.pallas_call(
        _proj_kernel,
        grid=(t // tm,),
        in_specs=([tok(D_MODEL)] + [spec for _, spec in consts] + [tokt(ROPE_HALF)] * 2
                  + [spec for _, spec in gains]
                  + [pl.BlockSpec((1, 1) + km.shape[2:], mem_map), pl.BlockSpec((1, 1) + vmt.shape[2:], mem_map)]),
        out_specs=[spec for spec, _ in outs],
        out_shape=[jax.ShapeDtypeStruct(shape, bf) for _, shape in outs],
        compiler_params=_params("arbitrary"),
        name="proj",
    )(x2, *[a for a, _ in consts], *tabs, *[a for a, _ in gains], km, vmt)


def _chunk_mask_t(shape):
    krow = lax.broadcasted_iota(jnp.int32, shape, 0)
    qcol = lax.broadcasted_iota(jnp.int32, shape, 1)
    return (krow // CHUNK) <= (qcol // CHUNK)


def _flash_step(n, score_fn, vt_fn, m_ref, acc_ref, online):
    def scores_and_max(i):
        s = score_fn(i)
        if not online:
            return s, None, None
        m_old = m_ref[i]
        m_new = jnp.maximum(m_old, jnp.max(s, axis=0, keepdims=True))
        m_ref[i] = m_new
        return s, m_new, jnp.exp2(m_old - m_new)

    def exponentials(a):
        s, m_new, alpha = a
        return _bf16(jnp.exp2(s if m_new is None else s - m_new)), alpha

    def value_product(i, b):
        p, alpha = b
        acc = acc_ref[i] if alpha is None else alpha * acc_ref[i]
        acc_ref[i] = acc + _dot(vt_fn(i), p)

    stage_a, stage_b = {}, {}
    for it in range(n + 2):
        if it < n:
            stage_a[it] = scores_and_max(it)
        if 0 <= it - 1 < n:
            stage_b[it - 1] = exponentials(stage_a.pop(it - 1))
        if 0 <= it - 2 < n:
            value_product(it - 2, stage_b.pop(it - 2))


def _softmax_modes(safe, m_ref, acc_ref, run):
    acc_ref[...] = jnp.zeros(acc_ref.shape, jnp.float32)

    @pl.when(safe == 1)
    def _():
        run(False)

    @pl.when(safe != 1)
    def _():
        m_ref[...] = jnp.full(m_ref.shape, NEG, jnp.float32)
        run(True)


def _key_tile(start_tile, n_tiles, ta):
    return pl.ds(pl.multiple_of(start_tile * ta, ta), n_tiles * ta)


def _for_unmasked_tiles(n, tile, group=FAR_TILES):
    def body(i, carry):
        tile(i * group, group)
        return carry

    n = jnp.maximum(n, 0)
    lax.fori_loop(0, n // group, body, 0)
    for r in range(1, group):
        @pl.when(n % group >= r)
        def _():
            tile(n - n % group + (r - 1), 1)


def _mla_attn_kernel(safe_ref, qt_ref, k_ref, vt_ref, o_ref, m_ref, acc_ref, *, layer):
    qi = pl.program_id(1)
    ta = qt_ref.shape[1]

    def run(online):
        def tile(start, n_tiles, masked=False):
            keys = _key_tile(start, n_tiles, ta)
            mask = _chunk_mask_t((ta, ta)) if masked else None

            def score(hd):
                sl = slice(hd * LANES, (hd + 1) * LANES)
                s = _dot(k_ref[keys, sl], qt_ref[sl, :])
                return jnp.where(mask, s, NEG) if masked else s

            _flash_step(MLA_HEADS, score,
                        lambda hd: vt_ref[hd * MLA_VROWS:(hd + 1) * MLA_VROWS, keys], m_ref, acc_ref, online)

        _for_unmasked_tiles(qi, tile, group=FAR_TILES if online else 1)
        tile(qi, 1, masked=True)

    _softmax_modes(safe_ref[layer], m_ref, acc_ref, run)
    outs = []
    for hd in range(MLA_HEADS):
        a = acc_ref[hd]
        outs.append(a[:MLA_V] / a[MLA_V:MLA_V + 1])
    o_ref[...] = _bf16(jnp.concatenate(outs, axis=0).T)


def _mla_attn(layer, safe, qt, k, vt, batch, seq):
    ta = ATTN_TILE
    nb = seq // ta
    return pl.pallas_call(
        functools.partial(_mla_attn_kernel, layer=layer),
        grid=(batch, nb),
        in_specs=[pl.BlockSpec(memory_space=pltpu.SMEM),
                  pl.BlockSpec((MLA_HEADS * LANES, ta), lambda b, qi: (0, b * nb + qi)),
                  pl.BlockSpec((seq, MLA_HEADS * LANES), lambda b, qi: (b, 0)),
                  pl.BlockSpec((MLA_HEADS * MLA_VROWS, seq), lambda b, qi: (0, b))],
        out_specs=pl.BlockSpec((ta, BRANCH_W), lambda b, qi: (b * nb + qi, 0)),
        out_shape=jax.ShapeDtypeStruct((batch * seq, BRANCH_W), jnp.bfloat16),
        scratch_shapes=[pltpu.VMEM((MLA_HEADS, 1, ta), jnp.float32),
                        pltpu.VMEM((MLA_HEADS, MLA_VROWS, ta), jnp.float32)],
        compiler_params=_params("arbitrary", "arbitrary"),
        name="mla_attn",
    )(safe, qt, k, vt)


def _diff_attn_kernel(safe_ref, ref_ref, lam_ref, q0t_ref, q1t_ref, k_ref, vt_ref,
                      bias_ref, gout_ref, o_ref, m_ref, acc_ref, *, layer, out_scale):
    qi = pl.program_id(1)
    ta = q0t_ref.shape[1]

    def run(online):
        def tile(start, n_tiles, bias_tile=None, masked=False):
            keys = _key_tile(start, n_tiles, ta)
            mask = _chunk_mask_t((ta, ta)) if masked else None

            def score(idx):
                hd, qt_ref = idx // 2, (q0t_ref, q1t_ref)[idx % 2]
                sl = slice(hd * LANES, (hd + 1) * LANES)
                s = _dot(k_ref[keys, sl], qt_ref[sl, :])
                if bias_tile is not None:
                    s = s + bias_ref[bias_tile, hd]
                if not online:
                    s = s - ref_ref[layer * DIFF_HEADS + hd]
                return jnp.where(mask, s, NEG) if masked else s

            _flash_step(2 * DIFF_HEADS, score,
                        lambda idx: vt_ref[(idx // 2) * DIFF_VROWS:(idx // 2 + 1) * DIFF_VROWS, keys],
                        m_ref, acc_ref, online)

        _for_unmasked_tiles(qi - 1, tile)

        @pl.when(qi > 0)
        def _():
            tile(qi - 1, 1, bias_tile=1)

        tile(qi, 1, bias_tile=0, masked=True)

    _softmax_modes(safe_ref[layer], m_ref, acc_ref, run)
    lam = lam_ref[layer]
    g = gout_ref[...] * out_scale
    outs = []
    for hd in range(DIFF_HEADS):
        a0, a1 = acc_ref[2 * hd], acc_ref[2 * hd + 1]
        o = a0[:DIFF_V] / a0[DIFF_V:DIFF_V + 1] - lam * (a1[:DIFF_V] / a1[DIFF_V:DIFF_V + 1])
        outs.append(o * _col_scale(_sumsq(o), DIFF_V) * g)
    o_ref[...] = _bf16(jnp.concatenate(outs, axis=0).T)


def _diff_attn(layer, safe, ref, lam, q0t, q1t, k, vt, bias, g_out, out_scale, batch, seq):
    ta = ATTN_TILE
    nb = seq // ta
    qt_spec = pl.BlockSpec((BRANCH_W, ta), lambda b, qi: (0, b * nb + qi))
    smem = pl.BlockSpec(memory_space=pltpu.SMEM)
    return pl.pallas_call(
        functools.partial(_diff_attn_kernel, layer=layer, out_scale=out_scale),
        grid=(batch, nb),
        in_specs=[smem, smem, smem, qt_spec, qt_spec,
                  pl.BlockSpec((seq, BRANCH_W), lambda b, qi: (b, 0)),
                  pl.BlockSpec((DIFF_HEADS * DIFF_VROWS, seq), lambda b, qi: (0, b)),
                  _single_spec(bias), _layer_spec(g_out, layer, single=True)],
        out_specs=pl.BlockSpec((ta, BRANCH_W), lambda b, qi: (b * nb + qi, 0)),
        out_shape=jax.ShapeDtypeStruct((batch * seq, BRANCH_W), jnp.bfloat16),
        scratch_shapes=[pltpu.VMEM((2 * DIFF_HEADS, 1, ta), jnp.float32),
                        pltpu.VMEM((2 * DIFF_HEADS, DIFF_VROWS, ta), jnp.float32)],
        compiler_params=_params("arbitrary", "arbitrary"),
        name="diff_attn",
    )(safe, ref, lam, q0t, q1t, k, vt, bias, g_out)


def _merge_mlp_kernel(x_ref, oa_ref, ob_ref, oc_ref, gmix_ref, wg_ref, wb_ref, wout_ref,
                      gmlp_ref, w1_ref, w2_ref, out_ref):
    x = x_ref[...]
    h = _bf16(x * _rms_scale(x, D_MODEL) * gmix_ref[...])
    y = None
    for n, o_ref in enumerate((oa_ref, ob_ref, oc_ref)):
        gate = 1.0 / (1.0 + jnp.exp(-_dot(h, wg_ref[:, n * D_MODEL:(n + 1) * D_MODEL])))
        term = gate * _dot(o_ref[...], wb_ref[n])
        y = term if y is None else y + term
    x = x + _dot(_bf16(y), wout_ref[...])
    h2 = _bf16(x * _rms_scale(x, D_MODEL) * gmlp_ref[...])
    for c in range(D_FF // FF_CHUNK):
        f = jnp.maximum(_dot(h2, w1_ref[:, c * FF_CHUNK:(c + 1) * FF_CHUNK]), 0.0)
        x = x + _dot(_bf16(f * f), w2_ref[c * FF_CHUNK:(c + 1) * FF_CHUNK, :])
    out_ref[...] = x


def _merge_mlp(layer, x2, o_a, o_b, o_c, w):
    t = x2.shape[0]
    tm = TOKEN_TILE
    tok = lambda c: pl.BlockSpec((tm, c), lambda i: (i, 0))
    consts = [w["g_mix"], w["w_gate"], w["w_branch"], w["w_out"], w["g_mlp"], w["w_ff1"], w["w_ff2"]]
    return pl.pallas_call(
        _merge_mlp_kernel,
        grid=(t // tm,),
        in_specs=([tok(D_MODEL)] + [tok(BRANCH_W)] * 3
                  + [_layer_spec(c, layer, single=True) for c in consts]),
        out_specs=tok(D_MODEL),
        out_shape=jax.ShapeDtypeStruct((t, D_MODEL), jnp.float32),
        compiler_params=_params("arbitrary"),
        name="merge_mlp",
    )(x2, o_a, o_b, o_c, *consts)


def _transposed_value_weight(w, rows):
    depth, n_in, heads, width = w.shape
    wt = jnp.pad(jnp.transpose(w, (0, 2, 3, 1)), ((0, 0), (0, 0), (0, rows - width), (0, 0)))
    return wt.reshape(depth, heads * rows, n_in)


def _ones_rows(heads, rows, at):
    col = np.zeros((heads, rows, 1), np.float32)
    col[:, at] = 1.0
    return jnp.asarray(col.reshape(heads * rows, 1))


def _layer_weights(w_in, g_mix, g_cq, w_uq, g_ckv, w_ukv, g_mla_q, g_mla_k, g_diff_q, g_diff_k,
                   g_mem_q, w_branch, w_out, g_mlp, w_ff1, w_ff2):
    splits = np.cumsum([Q_LORA, KV_LORA, MLA_ROPE, DIFF_HEADS * 2 * DIFF_QK, DIFF_HEADS * 2 * DIFF_QK,
                        DIFF_HEADS * DIFF_V, MEM_HEADS * MEM_HEAD_DIM]).tolist()
    c_dv, c_mq, c_gate = splits[4], splits[5], splits[6]
    t_in = jnp.swapaxes(w_in[:, :, :c_gate], 1, 2)
    w_dvt = _transposed_value_weight(
        w_in[:, :, c_dv:c_mq].reshape(DEPTH, D_MODEL, DIFF_HEADS, DIFF_V), DIFF_VROWS)
    w_t = jnp.concatenate([t_in[:, :c_dv], w_dvt, t_in[:, c_mq:]], axis=1)
    assert w_t.shape[1] == _R_END
    w_uqt = _transposed_value_weight(w_uq.reshape(DEPTH, Q_LORA, MLA_HEADS, MLA_QK), MLA_QK)
    w_ukv = w_ukv.reshape(DEPTH, KV_LORA, MLA_HEADS, MLA_NOPE + MLA_V)
    w_ukt = _transposed_value_weight(w_ukv[..., :MLA_NOPE], MLA_NOPE)
    w_uvt = _transposed_value_weight(w_ukv[..., MLA_NOPE:], MLA_VROWS)
    row = lambda g: g[:, None, :]
    col = lambda g: g[:, :, None]
    return {
        "g_mix": row(g_mix), "w_t": _bf16(w_t), "g_cq": col(g_cq), "w_uqt": _bf16(w_uqt),
        "g_ckv": col(g_ckv), "w_ukt": _bf16(w_ukt), "w_uvt": _bf16(w_uvt),
        "g_q": col(g_mla_q * (MLA_QK ** -0.5 * LOG2E)), "g_k": col(g_mla_k),
        "g_dq": col(g_diff_q * (DIFF_QK ** -0.5 * LOG2E)), "g_dk": col(g_diff_k),
        "g_mq": col(g_mem_q * MEM_HEAD_DIM ** -0.5),
        "w_gate": _bf16(w_in[:, :, c_gate:]), "w_branch": _bf16(w_branch), "w_out": _bf16(w_out),
        "g_mlp": row(g_mlp), "w_ff1": _bf16(w_ff1), "w_ff2": _bf16(w_ff2),
    }


def _logit_bound(g_q, g_k, n, scale):
    bound = (n * scale * LOG2E * BOUND_MARGIN) * jnp.max(jnp.abs(g_q), axis=-1) * jnp.max(jnp.abs(g_k), axis=-1)
    return _bf16(bound).astype(jnp.float32)


def _softmax_bounds(g_mla_q, g_mla_k, g_diff_q, g_diff_k, bias):
    r_mla = _logit_bound(g_mla_q, g_mla_k, MLA_QK, MLA_QK ** -0.5)
    bias_hi = jnp.max(bias, axis=(0, 2, 3))
    bias_range = jnp.max(bias_hi - jnp.min(bias, axis=(0, 2, 3)))
    r_diff = _logit_bound(g_diff_q, g_diff_k, DIFF_QK, DIFF_QK ** -0.5)
    safe = lambda spread: (spread <= SAFE_LOGIT_SPREAD).astype(jnp.int32)
    return (r_mla, safe(2.0 * r_mla),
            (r_diff[:, None] + bias_hi[None, :]).reshape(-1), safe(2.0 * r_diff + bias_range))


def kernel(x, mem, positions, t5_table, g_mix, g_mem, w_in, g_cq, w_uq, g_ckv, w_ukv, g_mla_q, g_mla_k, g_diff_q, g_diff_k, lam_q1, lam_k1, lam_q2, lam_k2, g_diff_out, w_mem_kv, g_mem_q, g_mem_k, w_branch, w_out, g_mlp, w_ff1, w_ff2):
    b, s, _ = x.shape
    assert s % ATTN_TILE == 0 and s % TOKEN_TILE == 0 and s % PROJ_TILE == 0 and (b * s) % TABLE_TILE == 0
    tabs = _rope_tables(positions)
    bias = _t5_bias_tiles(t5_table)
    km, vmt = _mem_kv(mem, g_mem, w_mem_kv, g_mem_k)
    weights = _layer_weights(w_in, g_mix, g_cq, w_uq, g_ckv, w_ukv, g_mla_q, g_mla_k, g_diff_q,
                             g_diff_k, g_mem_q, w_branch, w_out, g_mlp, w_ff1, w_ff2)
    f32 = jnp.float32
    lam_init = [0.8 - 0.6 * math.exp(-0.3 * l) for l in range(DEPTH)]
    lam = (jnp.exp(jnp.sum(lam_q1.astype(f32) * lam_k1.astype(f32), axis=-1))
           - jnp.exp(jnp.sum(lam_q2.astype(f32) * lam_k2.astype(f32), axis=-1))
           + jnp.asarray(lam_init, f32))
    x2 = x.reshape(b * s, D_MODEL)
    g_out = g_diff_out[:, :, None]
    r_mla, safe_mla, r_diff, safe_diff = _softmax_bounds(g_mla_q, g_mla_k, g_diff_q, g_diff_k, bias)
    weights["q_pad"] = jnp.pad(-r_mla[:, None, None], ((0, 0), (0, LANES - MLA_QK - 1), (0, 0)))
    for l in range(DEPTH):
        qt, k, vt, dq0t, dq1t, dk, dvt, o_c = _proj(l, x2, weights, tabs, km, vmt, s)
        o_a = _mla_attn(l, safe_mla, qt, k, vt, b, s)
        o_b = _diff_attn(l, safe_diff, r_diff, lam, dq0t, dq1t, dk, dvt, bias, g_out,
                         1.0 - lam_init[l], b, s)
        x2 = _merge_mlp(l, x2, o_a, o_b, o_c, weights)
    return x2.reshape(b, s, D_MODEL)
```

```python
import functools
import math

import jax
import jax.numpy as jnp
import numpy as np
from jax import lax
from jax.experimental import pallas as pl
from jax.experimental.pallas import tpu as pltpu

D_MODEL = 1024
DEPTH = 4
CHUNK = 64
MLA_HEADS = 8
MLA_NOPE = 64
MLA_ROPE = 32
MLA_V = 64
MLA_QK = MLA_NOPE + MLA_ROPE
Q_LORA = 384
KV_LORA = 256
ROPE_BASE = 10000.0
DIFF_HEADS = 4
DIFF_QK = 64
DIFF_V = 2 * DIFF_QK
MEM_HEADS = 4
MEM_HEAD_DIM = 128
BRANCH_W = 512
N_BRANCH = 3
D_FF = 4 * D_MODEL
T5_BUCKETS = 32
T5_MAX_DIST = 128
EPS = 1e-6
NEG = -1e30

LANES = 128
BF16_SUBLANES = 16
V7X_VMEM_BYTES = 64 * 1024 * 1024
VMEM_LIMIT = V7X_VMEM_BYTES * 7 // 8

TOKEN_TILE = 512
PROJ_TILE = 1024
ATTN_TILE = 512
FAR_TILES = 2
TABLE_TILE = 4096
FF_CHUNK = 1024
ROPE_HALF = MLA_ROPE // 2


def _round_up(n, m):
    return -(-n // m) * m


MLA_VROWS = _round_up(MLA_V + 1, BF16_SUBLANES)
DIFF_VROWS = _round_up(DIFF_V + 1, BF16_SUBLANES)
MEM_VROWS = _round_up(MEM_HEAD_DIM + 1, BF16_SUBLANES)
LOG2E = math.log2(math.e)

SAFE_LOGIT_SPREAD = 120.0
BOUND_MARGIN = 1.02

_R_CQ = 0
_R_CKV = _R_CQ + Q_LORA
_R_KR = _R_CKV + KV_LORA
_R_DQ = _R_KR + MLA_ROPE
_R_DK = _R_DQ + DIFF_HEADS * LANES
_R_DV = _R_DK + DIFF_HEADS * LANES
_R_MQ = _R_DV + DIFF_HEADS * DIFF_VROWS
_R_END = _R_MQ + MEM_HEADS * MEM_HEAD_DIM


def _bf16(x):
    return x.astype(jnp.bfloat16)


def _dot(a, b):
    return jnp.dot(a, b, preferred_element_type=jnp.float32)


def _dot_nt(a, b):
    return lax.dot_general(a, b, (((1,), (1,)), ((), ())), preferred_element_type=jnp.float32)


def _rms_scale(x, n):
    return lax.rsqrt(jnp.sum(x * x, axis=-1, keepdims=True) * (1.0 / n) + EPS)


def _sumsq(x):
    return jnp.sum(x * x, axis=0, keepdims=True)


def _col_scale(sumsq, n):
    return lax.rsqrt(sumsq * (1.0 / n) + EPS)


def _const_spec(shape):
    return pl.BlockSpec(shape, lambda *_: (0,) * len(shape))


def _single_spec(a):
    return pl.BlockSpec(a.shape, lambda *_: (0,) * a.ndim, pipeline_mode=pl.Buffered(1))


def _layer_spec(a, layer, single=False):
    return pl.BlockSpec((None,) + a.shape[1:], lambda *_: (layer,) + (0,) * (a.ndim - 1),
                        pipeline_mode=pl.Buffered(1) if single else None)


def _params(*sem):
    return pltpu.CompilerParams(dimension_semantics=sem, vmem_limit_bytes=VMEM_LIMIT)


def _rope_tables_kernel(pos_ref, inv_ref, cos_ref, sin_ref):
    ang = inv_ref[...] * pos_ref[...].astype(jnp.float32)
    cos_ref[...] = jnp.cos(ang)
    sin_ref[...] = jnp.sin(ang)


def _rope_tables(positions):
    t = positions.size
    inv = jnp.power(jnp.float32(ROPE_BASE), -jnp.arange(ROPE_HALF, dtype=jnp.float32) / ROPE_HALF)
    out = jax.ShapeDtypeStruct((ROPE_HALF, t), jnp.float32)
    return pl.pallas_call(
        _rope_tables_kernel,
        grid=(t // TABLE_TILE,),
        in_specs=[pl.BlockSpec((1, TABLE_TILE), lambda i: (0, i)), _const_spec((ROPE_HALF, 1))],
        out_specs=[pl.BlockSpec((ROPE_HALF, TABLE_TILE), lambda i: (0, i))] * 2,
        out_shape=[out] * 2,
        compiler_params=_params("arbitrary"),
        name="rope_tables",
    )(positions.reshape(1, t), inv.reshape(ROPE_HALF, 1))


def _t5_large_thresholds():
    n = T5_BUCKETS // 2
    max_exact = n // 2
    assert T5_MAX_DIST == 16 * max_exact and n - max_exact == 8
    out = []
    for j in range(1, n - max_exact):
        a = max_exact
        while a * a < max_exact * max_exact * 2 ** j:
            a += 1
        out.append(a)
    return out


def _t5_bias_kernel(table_ref, bias_ref):
    t = bias_ref.shape[-1]
    off = pl.program_id(0) * t
    row = lax.broadcasted_iota(jnp.int32, (t, t), 0)
    col = lax.broadcasted_iota(jnp.int32, (t, t), 1)
    rel = row - col - off
    n = T5_BUCKETS // 2
    max_exact = n // 2
    a = jnp.abs(rel)
    large = jnp.full((t, t), max_exact, jnp.int32)
    for thr in _t5_large_thresholds():
        large = large + (a >= thr).astype(jnp.int32)
    bucket = jnp.where(rel > 0, n, 0) + jnp.where(a < max_exact, a, large)
    far_bucket = n - 1
    for h in range(DIFF_HEADS):
        val = jnp.full((t, t), table_ref[0, h], jnp.float32)
        for b in range(1, T5_BUCKETS):
            val = jnp.where(bucket == b, table_ref[b, h], val)
        bias_ref[0, h] = (val - table_ref[far_bucket, h]) * LOG2E


def _t5_bias_tiles(t5_table):
    t = ATTN_TILE
    assert t >= _t5_large_thresholds()[-1]
    return pl.pallas_call(
        _t5_bias_kernel,
        grid=(2,),
        in_specs=[pl.BlockSpec(memory_space=pltpu.SMEM)],
        out_specs=pl.BlockSpec((1, DIFF_HEADS, t, t), lambda i: (i, 0, 0, 0)),
        out_shape=jax.ShapeDtypeStruct((2, DIFF_HEADS, t, t), jnp.float32),
        compiler_params=_params("arbitrary"),
        name="t5_bias_tiles",
    )(t5_table.astype(jnp.float32))


def _mem_kv_kernel(mem_ref, g_ref, wk_ref, wvt_ref, ones_ref, gk_ref, km_ref, vmt_ref):
    x = mem_ref[0]
    h = _bf16(x * _rms_scale(x, D_MODEL) * g_ref[0])
    k = _dot(h, wk_ref[0])
    for hd in range(MEM_HEADS):
        sl = slice(hd * MEM_HEAD_DIM, (hd + 1) * MEM_HEAD_DIM)
        kh = k[:, sl]
        km_ref[0, 0, :, sl] = _bf16(kh * _rms_scale(kh, MEM_HEAD_DIM) * gk_ref[0])
    vmt_ref[0, 0] = _bf16(_dot_nt(wvt_ref[0], h) + ones_ref[...])


def _mem_kv(mem, g_mem, w_mem_kv, g_mem_k):
    b, m, _ = mem.shape
    width = MEM_HEADS * MEM_HEAD_DIM
    rows = MEM_HEADS * MEM_VROWS
    w_k = _bf16(w_mem_kv[:, :, :width])
    w_vt = _bf16(_transposed_value_weight(
        w_mem_kv[:, :, width:].reshape(DEPTH, D_MODEL, MEM_HEADS, MEM_HEAD_DIM), MEM_VROWS))
    ones = _ones_rows(MEM_HEADS, MEM_VROWS, MEM_HEAD_DIM)
    layer = lambda l, i: (l, 0, 0)
    return pl.pallas_call(
        _mem_kv_kernel,
        grid=(DEPTH, b),
        in_specs=[
            pl.BlockSpec((1, m, D_MODEL), lambda l, i: (i, 0, 0)),
            pl.BlockSpec((1, 1, D_MODEL), layer),
            pl.BlockSpec((1, D_MODEL, width), layer),
            pl.BlockSpec((1, rows, D_MODEL), layer),
            _const_spec(ones.shape),
            pl.BlockSpec((1, 1, MEM_HEAD_DIM), layer),
        ],
        out_specs=[pl.BlockSpec((1, 1, m, width), lambda l, i: (l, i, 0, 0)),
                   pl.BlockSpec((1, 1, rows, m), lambda l, i: (l, i, 0, 0))],
        out_shape=[jax.ShapeDtypeStruct((DEPTH, b, m, width), jnp.bfloat16),
                   jax.ShapeDtypeStruct((DEPTH, b, rows, m), jnp.bfloat16)],
        compiler_params=_params("arbitrary", "arbitrary"),
        name="mem_kv",
    )(mem, g_mem.reshape(DEPTH, 1, D_MODEL), w_k, w_vt, ones, g_mem_k.reshape(DEPTH, 1, MEM_HEAD_DIM))


def _rope_rows(x1, x2, cos_t, sin_t):
    return x1 * cos_t - x2 * sin_t, x2 * cos_t + x1 * sin_t


def _proj_kernel(x_ref, gmix_ref, wt_ref, gcq_ref, wuqt_ref, gckv_ref, wukt_ref, wuvt_ref, vones_ref,
                 dvones_ref, gq_ref, gk_ref, qpad_ref, kpad_ref, cos_ref, sin_ref, gdq_ref, gdk_ref, gmq_ref,
                 km_ref, vmt_ref,
                 qt_ref, k_ref, vt_ref, dq0t_ref, dq1t_ref, dk_ref, dvt_ref, oc_ref):
    x = x_ref[...]
    tm = x.shape[0]
    h = _bf16(x * _rms_scale(x, D_MODEL) * gmix_ref[...])
    zt = _dot_nt(wt_ref[...], h)
    cos_t, sin_t = cos_ref[...], sin_ref[...]

    c_q = zt[_R_CQ:_R_CKV]
    cq_n = _bf16(c_q * _col_scale(_sumsq(c_q), Q_LORA) * gcq_ref[...])
    qt_raw = _dot(wuqt_ref[...], cq_n)
    c_kv = zt[_R_CKV:_R_KR]
    ckv_n = _bf16(c_kv * _col_scale(_sumsq(c_kv), KV_LORA) * gckv_ref[...])
    k_nope = _dot(wukt_ref[...], ckv_n)
    vt_ref[...] = _bf16(_dot(wuvt_ref[...], ckv_n) + vones_ref[...])
    gq, gk = gq_ref[...], gk_ref[...]
    k_r = zt[_R_KR:_R_DQ]
    kr_ss = _sumsq(k_r)
    kr_g = k_r * gk[MLA_NOPE:]
    kr1, kr2 = _rope_rows(kr_g[:ROPE_HALF], kr_g[ROPE_HALF:], cos_t, sin_t)
    q_pad = jnp.broadcast_to(qpad_ref[...], (LANES - MLA_QK, tm))
    k_pad = jnp.broadcast_to(kpad_ref[...], (LANES - MLA_QK, tm))
    for hd in range(MLA_HEADS):
        rows = slice(hd * LANES, (hd + 1) * LANES)
        q = qt_raw[hd * MLA_QK:(hd + 1) * MLA_QK]
        qn = q * _col_scale(_sumsq(q), MLA_QK) * gq
        q1, q2 = _rope_rows(qn[MLA_NOPE:MLA_NOPE + ROPE_HALF], qn[MLA_NOPE + ROPE_HALF:], cos_t, sin_t)
        qt_ref[rows, :] = _bf16(jnp.concatenate([qn[:MLA_NOPE], q1, q2, q_pad], axis=0))
        kn = k_nope[hd * MLA_NOPE:(hd + 1) * MLA_NOPE]
        rs = _col_scale(_sumsq(kn) + kr_ss, MLA_QK)
        kt = jnp.concatenate([kn * rs * gk[:MLA_NOPE], kr1 * rs, kr2 * rs, k_pad], axis=0)
        k_ref[:, rows] = _bf16(kt.T)

    gdq, gdk = gdq_ref[...], gdk_ref[...]
    zero_half = jnp.zeros((DIFF_QK, tm), jnp.float32)

    def half_norm(t, g):
        return t * _col_scale(_sumsq(t), DIFF_QK) * g

    for hd in range(DIFF_HEADS):
        rows = slice(hd * LANES, (hd + 1) * LANES)
        r0 = _R_DQ + hd * LANES
        c0 = half_norm(zt[r0:r0 + DIFF_QK], gdq)
        c1 = half_norm(zt[r0 + DIFF_QK:r0 + LANES], gdq)
        dq0t_ref[rows, :] = _bf16(jnp.concatenate([c0, zero_half], axis=0))
        dq1t_ref[rows, :] = _bf16(jnp.concatenate([zero_half, c1], axis=0))
        r0 = _R_DK + hd * LANES
        k0, k1 = half_norm(zt[r0:r0 + DIFF_QK], gdk), half_norm(zt[r0 + DIFF_QK:r0 + LANES], gdk)
        dk_ref[:, rows] = _bf16(jnp.concatenate([k0, k1], axis=0).T)
    dvt_ref[...] = _bf16(zt[_R_DV:_R_MQ] + dvones_ref[...])

    gmq = gmq_ref[...]
    for hd in range(MEM_HEADS):
        sl = slice(hd * MEM_HEAD_DIM, (hd + 1) * MEM_HEAD_DIM)
        mq = zt[_R_MQ + hd * MEM_HEAD_DIM:_R_MQ + (hd + 1) * MEM_HEAD_DIM]
        mq = _bf16(mq * _col_scale(_sumsq(mq), MEM_HEAD_DIM) * gmq)
        s = _dot(km_ref[0, 0, :, sl], mq)
        p = _bf16(jnp.exp(s - jnp.max(s, axis=0, keepdims=True)))
        o = _dot(vmt_ref[0, 0, hd * MEM_VROWS:(hd + 1) * MEM_VROWS, :], p)
        o = o[:MEM_HEAD_DIM] / o[MEM_HEAD_DIM:MEM_HEAD_DIM + 1]
        oc_ref[:, sl] = _bf16(o.T)


def _proj(layer, x2, w, tabs, km, vmt, seq):
    t = x2.shape[0]
    tm = PROJ_TILE
    steps_per_batch = seq // tm
    tok = lambda c: pl.BlockSpec((tm, c), lambda i: (i, 0))
    tokt = lambda r: pl.BlockSpec((r, tm), lambda i: (0, i))
    mem_map = lambda i: (layer, i // steps_per_batch, 0, 0)
    bf = jnp.bfloat16
    stacked = lambda name: (w[name], _layer_spec(w[name], layer))
    fixed = lambda a: (a, _const_spec(a.shape))
    consts = [stacked("g_mix"), stacked("w_t"), stacked("g_cq"), stacked("w_uqt"), stacked("g_ckv"),
              stacked("w_ukt"), stacked("w_uvt"), fixed(_ones_rows(MLA_HEADS, MLA_VROWS, MLA_V)),
              fixed(_ones_rows(DIFF_HEADS, DIFF_VROWS, DIFF_V)), stacked("g_q"), stacked("g_k"),
              stacked("q_pad"), fixed(_ones_rows(1, LANES - MLA_QK, 0))]
    gains = [stacked("g_dq"), stacked("g_dk"), stacked("g_mq")]
    wide, vt_rows, dvt_rows = MLA_HEADS * LANES, MLA_HEADS * MLA_VROWS, DIFF_HEADS * DIFF_VROWS
    outs = [(tokt(wide), (wide, t)), (tok(wide), (t, wide)), (tokt(vt_rows), (vt_rows, t)),
            (tokt(BRANCH_W), (BRANCH_W, t)), (tokt(BRANCH_W), (BRANCH_W, t)), (tok(BRANCH_W), (t, BRANCH_W)),
            (tokt(dvt_rows), (dvt_rows, t)), (tok(BRANCH_W), (t, BRANCH_W))]
    return pl.pallas_call(
        _proj_kernel,
        grid=(t // tm,),
        in_specs=([tok(D_MODEL)] + [spec for _, spec in consts] + [tokt(ROPE_HALF)] * 2
                  + [spec for _, spec in gains]
                  + [pl.BlockSpec((1, 1) + km.shape[2:], mem_map), pl.BlockSpec((1, 1) + vmt.shape[2:], mem_map)]),
        out_specs=[spec for spec, _ in outs],
        out_shape=[jax.ShapeDtypeStruct(shape, bf) for _, shape in outs],
        compiler_params=_params("arbitrary"),
        name="proj",
    )(x2, *[a for a, _ in consts], *tabs, *[a for a, _ in gains], km, vmt)


def _chunk_mask_t(shape):
    krow = lax.broadcasted_iota(jnp.int32, shape, 0)
    qcol = lax.broadcasted_iota(jnp.int32, shape, 1)
    return (krow // CHUNK) <= (qcol // CHUNK)


def _flash_step(n, score_fn, vt_fn, m_ref, acc_ref, online, first=False):
    def scores_and_max(i):
        s = score_fn(i)
        if not online:
            return s, None, None
        m_old = m_ref[i]
        m_new = jnp.maximum(m_old, jnp.max(s, axis=0, keepdims=True))
        m_ref[i] = m_new
        return s, m_new, jnp.exp2(m_old - m_new)

    def exponentials(a):
        s, m_new, alpha = a
        return _bf16(jnp.exp2(s if m_new is None else s - m_new)), alpha

    def value_product(i, b):
        p, alpha = b
        pv = _dot(vt_fn(i), p)
        if first:
            acc_ref[i] = pv
        else:
            acc_ref[i] = (acc_ref[i] if alpha is None else alpha * acc_ref[i]) + pv

    stage_a, stage_b = {}, {}
    for it in range(n + 2):
        if it < n:
            stage_a[it] = scores_and_max(it)
        if 0 <= it - 1 < n:
            stage_b[it - 1] = exponentials(stage_a.pop(it - 1))
        if 0 <= it - 2 < n:
            value_product(it - 2, stage_b.pop(it - 2))


def _softmax_modes(safe, m_ref, acc_ref, run):
    @pl.when(safe == 1)
    def _():
        run(False)

    @pl.when(safe != 1)
    def _():
        acc_ref[...] = jnp.zeros(acc_ref.shape, jnp.float32)
        m_ref[...] = jnp.full(m_ref.shape, NEG, jnp.float32)
        run(True)


def _key_tile(start_tile, n_tiles, ta):
    return pl.ds(pl.multiple_of(start_tile * ta, ta), n_tiles * ta)


def _for_unmasked_tiles(n, tile):
    def body(i, carry):
        tile(i * FAR_TILES, FAR_TILES)
        return carry

    n = jnp.maximum(n, 0)
    lax.fori_loop(0, n // FAR_TILES, body, 0)
    for r in range(1, FAR_TILES):
        @pl.when(n % FAR_TILES >= r)
        def _():
            tile(n - n % FAR_TILES + (r - 1), 1)


def _mla_attn_kernel(safe_ref, qt_ref, k_ref, vt_ref, o_ref, m_ref, acc_ref, *, layer):
    qi = pl.program_id(1)
    ta = qt_ref.shape[1]

    def run(online):
        def tile(start, n_tiles, masked=False):
            keys = _key_tile(start, n_tiles, ta)
            mask = _chunk_mask_t((ta, ta)) if masked else None

            def score(hd):
                sl = slice(hd * LANES, (hd + 1) * LANES)
                s = _dot(k_ref[keys, sl], qt_ref[sl, :])
                return jnp.where(mask, s, NEG) if masked else s

            _flash_step(MLA_HEADS, score,
                        lambda hd: vt_ref[hd * MLA_VROWS:(hd + 1) * MLA_VROWS, keys], m_ref, acc_ref, online,
                        first=masked and not online)

        if online:
            _for_unmasked_tiles(qi, tile)
            tile(qi, 1, masked=True)
        else:
            tile(qi, 1, masked=True)
            _for_unmasked_tiles(qi, tile)

    _softmax_modes(safe_ref[layer], m_ref, acc_ref, run)
    outs = []
    for hd in range(MLA_HEADS):
        a = acc_ref[hd]
        outs.append(a[:MLA_V] / a[MLA_V:MLA_V + 1])
    o_ref[...] = _bf16(jnp.concatenate(outs, axis=0).T)


def _mla_attn(layer, safe, qt, k, vt, batch, seq):
    ta = ATTN_TILE
    nb = seq // ta
    return pl.pallas_call(
        functools.partial(_mla_attn_kernel, layer=layer),
        grid=(batch, nb),
        in_specs=[pl.BlockSpec(memory_space=pltpu.SMEM),
                  pl.BlockSpec((MLA_HEADS * LANES, ta), lambda b, qi: (0, b * nb + qi)),
                  pl.BlockSpec((seq, MLA_HEADS * LANES), lambda b, qi: (b, 0)),
                  pl.BlockSpec((MLA_HEADS * MLA_VROWS, seq), lambda b, qi: (0, b))],
        out_specs=pl.BlockSpec((ta, BRANCH_W), lambda b, qi: (b * nb + qi, 0)),
        out_shape=jax.ShapeDtypeStruct((batch * seq, BRANCH_W), jnp.bfloat16),
        scratch_shapes=[pltpu.VMEM((MLA_HEADS, 1, ta), jnp.float32),
                        pltpu.VMEM((MLA_HEADS, MLA_VROWS, ta), jnp.float32)],
        compiler_params=_params("arbitrary", "arbitrary"),
        name="mla_attn",
    )(safe, qt, k, vt)


def _diff_attn_kernel(safe_ref, ref_ref, lam_ref, q0t_ref, q1t_ref, k_ref, vt_ref,
                      bias_ref, gout_ref, o_ref, m_ref, acc_ref, *, layer, out_scale):
    qi = pl.program_id(1)
    ta = q0t_ref.shape[1]

    def run(online):
        def tile(start, n_tiles, bias_tile=None, masked=False):
            keys = _key_tile(start, n_tiles, ta)
            mask = _chunk_mask_t((ta, ta)) if masked else None

            def score(idx):
                hd, qt_ref = idx // 2, (q0t_ref, q1t_ref)[idx % 2]
                sl = slice(hd * LANES, (hd + 1) * LANES)
                s = _dot(k_ref[keys, sl], qt_ref[sl, :])
                if bias_tile is not None:
                    s = s + bias_ref[bias_tile, hd]
                if not online:
                    s = s - ref_ref[layer * DIFF_HEADS + hd]
                return jnp.where(mask, s, NEG) if masked else s

            _flash_step(2 * DIFF_HEADS, score,
                        lambda idx: vt_ref[(idx // 2) * DIFF_VROWS:(idx // 2 + 1) * DIFF_VROWS, keys],
                        m_ref, acc_ref, online, first=masked and not online)

        if not online:
            tile(qi, 1, bias_tile=0, masked=True)
        _for_unmasked_tiles(qi - 1, tile)

        @pl.when(qi > 0)
        def _():
            tile(qi - 1, 1, bias_tile=1)

        if online:
            tile(qi, 1, bias_tile=0, masked=True)

    _softmax_modes(safe_ref[layer], m_ref, acc_ref, run)
    lam = lam_ref[layer]
    g = gout_ref[...] * out_scale
    outs = []
    for hd in range(DIFF_HEADS):
        a0, a1 = acc_ref[2 * hd], acc_ref[2 * hd + 1]
        o = a0[:DIFF_V] / a0[DIFF_V:DIFF_V + 1] - lam * (a1[:DIFF_V] / a1[DIFF_V:DIFF_V + 1])
        outs.append(o * _col_scale(_sumsq(o), DIFF_V) * g)
    o_ref[...] = _bf16(jnp.concatenate(outs, axis=0).T)


def _diff_attn(layer, safe, ref, lam, q0t, q1t, k, vt, bias, g_out, out_scale, batch, seq):
    ta = ATTN_TILE
    nb = seq // ta
    qt_spec = pl.BlockSpec((BRANCH_W, ta), lambda b, qi: (0, b * nb + qi))
    smem = pl.BlockSpec(memory_space=pltpu.SMEM)
    return pl.pallas_call(
        functools.partial(_diff_attn_kernel, layer=layer, out_scale=out_scale),
        grid=(batch, nb),
        in_specs=[smem, smem, smem, qt_spec, qt_spec,
                  pl.BlockSpec((seq, BRANCH_W), lambda b, qi: (b, 0)),
                  pl.BlockSpec((DIFF_HEADS * DIFF_VROWS, seq), lambda b, qi: (0, b)),
                  _single_spec(bias), _layer_spec(g_out, layer, single=True)],
        out_specs=pl.BlockSpec((ta, BRANCH_W), lambda b, qi: (b * nb + qi, 0)),
        out_shape=jax.ShapeDtypeStruct((batch * seq, BRANCH_W), jnp.bfloat16),
        scratch_shapes=[pltpu.VMEM((2 * DIFF_HEADS, 1, ta), jnp.float32),
                        pltpu.VMEM((2 * DIFF_HEADS, DIFF_VROWS, ta), jnp.float32)],
        compiler_params=_params("arbitrary", "arbitrary"),
        name="diff_attn",
    )(safe, ref, lam, q0t, q1t, k, vt, bias, g_out)


def _merge_mlp_kernel(x_ref, oa_ref, ob_ref, oc_ref, gmix_ref, wg_ref, wb_ref, wout_ref,
                      gmlp_ref, w1_ref, w2_ref, out_ref):
    x = x_ref[...]
    h = _bf16(x * _rms_scale(x, D_MODEL) * gmix_ref[...])
    y = None
    for n, o_ref in enumerate((oa_ref, ob_ref, oc_ref)):
        gate = 1.0 / (1.0 + jnp.exp(-_dot(h, wg_ref[:, n * D_MODEL:(n + 1) * D_MODEL])))
        term = gate * _dot(o_ref[...], wb_ref[n])
        y = term if y is None else y + term
    x = x + _dot(_bf16(y), wout_ref[...])
    h2 = _bf16(x * _rms_scale(x, D_MODEL) * gmlp_ref[...])
    for c in range(D_FF // FF_CHUNK):
        f = jnp.maximum(_dot(h2, w1_ref[:, c * FF_CHUNK:(c + 1) * FF_CHUNK]), 0.0)
        x = x + _dot(_bf16(f * f), w2_ref[c * FF_CHUNK:(c + 1) * FF_CHUNK, :])
    out_ref[...] = x


def _merge_mlp(layer, x2, o_a, o_b, o_c, w):
    t = x2.shape[0]
    tm = TOKEN_TILE
    tok = lambda c: pl.BlockSpec((tm, c), lambda i: (i, 0))
    consts = [w["g_mix"], w["w_gate"], w["w_branch"], w["w_out"], w["g_mlp"], w["w_ff1"], w["w_ff2"]]
    return pl.pallas_call(
        _merge_mlp_kernel,
        grid=(t // tm,),
        in_specs=([tok(D_MODEL)] + [tok(BRANCH_W)] * 3
                  + [_layer_spec(c, layer, single=True) for c in consts]),
        out_specs=tok(D_MODEL),
        out_shape=jax.ShapeDtypeStruct((t, D_MODEL), jnp.float32),
        compiler_params=_params("arbitrary"),
        name="merge_mlp",
    )(x2, o_a, o_b, o_c, *consts)


def _transposed_value_weight(w, rows):
    depth, n_in, heads, width = w.shape
    wt = jnp.pad(jnp.transpose(w, (0, 2, 3, 1)), ((0, 0), (0, 0), (0, rows - width), (0, 0)))
    return wt.reshape(depth, heads * rows, n_in)


def _ones_rows(heads, rows, at):
    col = np.zeros((heads, rows, 1), np.float32)
    col[:, at] = 1.0
    return jnp.asarray(col.reshape(heads * rows, 1))


def _layer_weights(w_in, g_mix, g_cq, w_uq, g_ckv, w_ukv, g_mla_q, g_mla_k, g_diff_q, g_diff_k,
                   g_mem_q, w_branch, w_out, g_mlp, w_ff1, w_ff2):
    splits = np.cumsum([Q_LORA, KV_LORA, MLA_ROPE, DIFF_HEADS * 2 * DIFF_QK, DIFF_HEADS * 2 * DIFF_QK,
                        DIFF_HEADS * DIFF_V, MEM_HEADS * MEM_HEAD_DIM]).tolist()
    c_dv, c_mq, c_gate = splits[4], splits[5], splits[6]
    t_in = jnp.swapaxes(w_in[:, :, :c_gate], 1, 2)
    w_dvt = _transposed_value_weight(
        w_in[:, :, c_dv:c_mq].reshape(DEPTH, D_MODEL, DIFF_HEADS, DIFF_V), DIFF_VROWS)
    w_t = jnp.concatenate([t_in[:, :c_dv], w_dvt, t_in[:, c_mq:]], axis=1)
    assert w_t.shape[1] == _R_END
    w_uqt = _transposed_value_weight(w_uq.reshape(DEPTH, Q_LORA, MLA_HEADS, MLA_QK), MLA_QK)
    w_ukv = w_ukv.reshape(DEPTH, KV_LORA, MLA_HEADS, MLA_NOPE + MLA_V)
    w_ukt = _transposed_value_weight(w_ukv[..., :MLA_NOPE], MLA_NOPE)
    w_uvt = _transposed_value_weight(w_ukv[..., MLA_NOPE:], MLA_VROWS)
    row = lambda g: g[:, None, :]
    col = lambda g: g[:, :, None]
    return {
        "g_mix": row(g_mix), "w_t": _bf16(w_t), "g_cq": col(g_cq), "w_uqt": _bf16(w_uqt),
        "g_ckv": col(g_ckv), "w_ukt": _bf16(w_ukt), "w_uvt": _bf16(w_uvt),
        "g_q": col(g_mla_q * (MLA_QK ** -0.5 * LOG2E)), "g_k": col(g_mla_k),
        "g_dq": col(g_diff_q * (DIFF_QK ** -0.5 * LOG2E)), "g_dk": col(g_diff_k),
        "g_mq": col(g_mem_q * MEM_HEAD_DIM ** -0.5),
        "w_gate": _bf16(w_in[:, :, c_gate:]), "w_branch": _bf16(w_branch), "w_out": _bf16(w_out),
        "g_mlp": row(g_mlp), "w_ff1": _bf16(w_ff1), "w_ff2": _bf16(w_ff2),
    }


def _logit_bound(g_q, g_k, n, scale):
    bound = (n * scale * LOG2E * BOUND_MARGIN) * jnp.max(jnp.abs(g_q), axis=-1) * jnp.max(jnp.abs(g_k), axis=-1)
    return _bf16(bound).astype(jnp.float32)


def _softmax_bounds(g_mla_q, g_mla_k, g_diff_q, g_diff_k, bias):
    r_mla = _logit_bound(g_mla_q, g_mla_k, MLA_QK, MLA_QK ** -0.5)
    bias_hi = jnp.max(bias, axis=(0, 2, 3))
    bias_range = jnp.max(bias_hi - jnp.min(bias, axis=(0, 2, 3)))
    r_diff = _logit_bound(g_diff_q, g_diff_k, DIFF_QK, DIFF_QK ** -0.5)
    safe = lambda spread: (spread <= SAFE_LOGIT_SPREAD).astype(jnp.int32)
    return (r_mla, safe(2.0 * r_mla),
            (r_diff[:, None] + bias_hi[None, :]).reshape(-1), safe(2.0 * r_diff + bias_range))


def kernel(x, mem, positions, t5_table, g_mix, g_mem, w_in, g_cq, w_uq, g_ckv, w_ukv, g_mla_q, g_mla_k, g_diff_q, g_diff_k, lam_q1, lam_k1, lam_q2, lam_k2, g_diff_out, w_mem_kv, g_mem_q, g_mem_k, w_branch, w_out, g_mlp, w_ff1, w_ff2):
    b, s, _ = x.shape
    assert s % ATTN_TILE == 0 and s % TOKEN_TILE == 0 and s % PROJ_TILE == 0 and (b * s) % TABLE_TILE == 0
    tabs = _rope_tables(positions)
    bias = _t5_bias_tiles(t5_table)
    km, vmt = _mem_kv(mem, g_mem, w_mem_kv, g_mem_k)
    weights = _layer_weights(w_in, g_mix, g_cq, w_uq, g_ckv, w_ukv, g_mla_q, g_mla_k, g_diff_q,
                             g_diff_k, g_mem_q, w_branch, w_out, g_mlp, w_ff1, w_ff2)
    f32 = jnp.float32
    lam_init = [0.8 - 0.6 * math.exp(-0.3 * l) for l in range(DEPTH)]
    lam = (jnp.exp(jnp.sum(lam_q1.astype(f32) * lam_k1.astype(f32), axis=-1))
           - jnp.exp(jnp.sum(lam_q2.astype(f32) * lam_k2.astype(f32), axis=-1))
           + jnp.asarray(lam_init, f32))
    x2 = x.reshape(b * s, D_MODEL)
    g_out = g_diff_out[:, :, None]
    r_mla, safe_mla, r_diff, safe_diff = _softmax_bounds(g_mla_q, g_mla_k, g_diff_q, g_diff_k, bias)
    weights["q_pad"] = jnp.pad(-r_mla[:, None, None], ((0, 0), (0, LANES - MLA_QK - 1), (0, 0)))
    for l in range(DEPTH):
        qt, k, vt, dq0t, dq1t, dk, dvt, o_c = _proj(l, x2, weights, tabs, km, vmt, s)
        o_a = _mla_attn(l, safe_mla, qt, k, vt, b, s)
        o_b = _diff_attn(l, safe_diff, r_diff, lam, dq0t, dq1t, dk, dvt, bias, g_out,
                         1.0 - lam_init[l], b, s)
        x2 = _merge_mlp(l, x2, o_a, o_b, o_c, weights)
    return x2.reshape(b, s, D_MODEL)
```
